```python
import jax, jax.numpy as jnp
from jax import lax
import numpy as np

D_MODEL = 1024
BATCH = 8
SEQ = 4096
DEPTH = 1

CONV_CH = D_MODEL
CONV_WIDTH = 31
HEAD_DIM = 128
HEADS_PER_GROUP = 4
ATTN_GROUPS = ((128, 1), (512, 4), (2048, 16))
N_GROUPS = len(ATTN_GROUPS)
N_ATTN_HEADS = HEADS_PER_GROUP * N_GROUPS
ATTN_WIDTH = N_ATTN_HEADS * HEAD_DIM
ATTN_OUT_WIDTH = HEADS_PER_GROUP * HEAD_DIM
BLOCK = 128
D_FF = 2816
EPS = 1e-6
IN_WIDTH = 2 * CONV_CH + 3 * ATTN_WIDTH + 2 * D_MODEL

kernel_name = "hybrid_conformer_dilated_attn_block"


def _alibi_slopes():
    h = np.arange(1, N_ATTN_HEADS + 1, dtype=np.float32)
    return np.power(np.float32(2.0), -8.0 * h / np.float32(N_ATTN_HEADS)).astype(np.float32)


def rmsnorm(x, g):
    xf = x.astype(jnp.float32)
    y = xf * lax.rsqrt(jnp.mean(xf * xf, axis=-1, keepdims=True) + EPS)
    return (y * g.astype(jnp.float32)).astype(x.dtype)


def swiglu_ffn(h, w_gate, w_up, w_down):
    return (jax.nn.silu(h @ w_gate) * (h @ w_up)) @ w_down


def conformer_conv(u, dw_kernel, dw_bias, ln_gain, ln_bias, w_out):
    a, b = jnp.split(u, 2, axis=-1)
    z = a * jax.nn.sigmoid(b)
    z = lax.conv_general_dilated(
        z, dw_kernel[:, None, :], window_strides=(1,),
        padding=[(CONV_WIDTH - 1, 0)],
        dimension_numbers=('NWC', 'WIO', 'NWC'),
        feature_group_count=CONV_CH) + dw_bias
    zf = z.astype(jnp.float32)
    mu = jnp.mean(zf, axis=-1, keepdims=True)
    var = jnp.mean(jnp.square(zf - mu), axis=-1, keepdims=True)
    zf = (zf - mu) * lax.rsqrt(var + EPS) * ln_gain.astype(jnp.float32) + ln_bias.astype(jnp.float32)
    z = jax.nn.silu(zf).astype(u.dtype)
    return z @ w_out


def dilated_group_attention(q, k, v, slopes, window, dilation):
    B, S, H, E = q.shape
    span = window // dilation
    chunk = dilation * BLOCK
    s_pad = -(-S // chunk) * chunk
    L = s_pad // dilation
    nb = L // BLOCK

    def to_blocks(t):
        t = jnp.pad(t, ((0, 0), (0, s_pad - S), (0, 0), (0, 0)))
        t = t.reshape(B, L, dilation, H, E)
        t = t.transpose(0, 3, 2, 1, 4)
        return t.reshape(B, H, dilation, nb, BLOCK, E)

    def with_prev(t):
        prev = jnp.pad(t[:, :, :, :-1], ((0, 0), (0, 0), (0, 0), (1, 0), (0, 0), (0, 0)))
        return jnp.concatenate([prev, t], axis=4)

    qb = to_blocks(q)
    kc = with_prev(to_blocks(k))
    vc = with_prev(to_blocks(v))

    scores = jnp.einsum('bhrnqe,bhrnke->bhrnqk', qb, kc).astype(jnp.float32) * (E ** -0.5)
    qi = jnp.arange(BLOCK)[:, None]
    ki = jnp.arange(2 * BLOCK)[None, :]
    steps = BLOCK + qi - ki
    in_band = (steps >= 0) & (steps <= span)
    first_block = (jnp.arange(nb) == 0)[:, None, None]
    valid = in_band[None] & ~(first_block & (ki < BLOCK)[None])
    bias = -slopes[:, None, None] * (dilation * steps).astype(jnp.float32)
    scores = scores + bias[None, :, None, None]
    scores = jnp.where(valid[None, None, None], scores, -jnp.inf)
    m = jnp.max(scores, axis=-1, keepdims=True)
    p = jnp.exp(scores - m)
    denom = jnp.sum(p, axis=-1, keepdims=True)
    out = jnp.einsum('bhrnqk,bhrnke->bhrnqe', p.astype(vc.dtype), vc).astype(jnp.float32) / denom
    lse = (m + jnp.log(denom))[..., 0]

    out = out.astype(q.dtype).reshape(B, H, dilation, L, E).transpose(0, 3, 2, 1, 4)
    out = out.reshape(B, s_pad, H, E)[:, :S]
    lse = lse.reshape(B, H, dilation, L).transpose(0, 3, 2, 1).reshape(B, s_pad, H)[:, :S]
    return out, lse


def setup_inputs(seed: int = 0) -> dict:
    key = jax.random.key(seed)
    ks = jax.random.split(key, 24)

    def w(k, shape, fan_in):
        return jax.random.normal(k, shape, jnp.float32) * (fan_in ** -0.5)

    def gain(k, shape):
        return 1.0 + 0.02 * jax.random.normal(k, shape, jnp.float32)

    def small(k, shape):
        return 0.02 * jax.random.normal(k, shape, jnp.float32)

    Lr = DEPTH
    return {
        "x": jax.random.normal(ks[0], (BATCH, SEQ, D_MODEL), jnp.float32),
        "ffn1_norm": gain(ks[1], (Lr, D_MODEL)),
        "ffn1_w_gate": w(ks[2], (Lr, D_MODEL, D_FF), D_MODEL),
        "ffn1_w_up": w(ks[3], (Lr, D_MODEL, D_FF), D_MODEL),
        "ffn1_w_down": w(ks[4], (Lr, D_FF, D_MODEL), D_FF),
        "mix_norm": gain(ks[5], (Lr, D_MODEL)),
        "w_in": w(ks[6], (Lr, D_MODEL, IN_WIDTH), D_MODEL),
        "conv_dw_kernel": w(ks[7], (Lr, CONV_WIDTH, CONV_CH), CONV_WIDTH),
        "conv_dw_bias": small(ks[8], (Lr, CONV_CH)),
        "conv_ln_gain": gain(ks[9], (Lr, CONV_CH)),
        "conv_ln_bias": small(ks[10], (Lr, CONV_CH)),
        "conv_w_out": w(ks[11], (Lr, CONV_CH, D_MODEL), CONV_CH),
        "attn_w_out": w(ks[12], (Lr, ATTN_OUT_WIDTH, D_MODEL), ATTN_OUT_WIDTH),
        "w_o": w(ks[13], (Lr, D_MODEL, D_MODEL), D_MODEL),
        "ffn2_norm": gain(ks[14], (Lr, D_MODEL)),
        "ffn2_w_gate": w(ks[15], (Lr, D_MODEL, D_FF), D_MODEL),
        "ffn2_w_up": w(ks[16], (Lr, D_MODEL, D_FF), D_MODEL),
        "ffn2_w_down": w(ks[17], (Lr, D_FF, D_MODEL), D_FF),
        "final_norm": gain(ks[18], (D_MODEL,)),
    }


def reference(x, ffn1_norm, ffn1_w_gate, ffn1_w_up, ffn1_w_down, mix_norm, w_in,
              conv_dw_kernel, conv_dw_bias, conv_ln_gain, conv_ln_bias, conv_w_out,
              attn_w_out, w_o, ffn2_norm, ffn2_w_gate, ffn2_w_up, ffn2_w_down, final_norm):
    B, S, _ = x.shape
    slopes = jnp.asarray(_alibi_slopes())
    split_at = [2 * CONV_CH,
                2 * CONV_CH + ATTN_WIDTH,
                2 * CONV_CH + 2 * ATTN_WIDTH,
                2 * CONV_CH + 3 * ATTN_WIDTH,
                2 * CONV_CH + 3 * ATTN_WIDTH + D_MODEL]
    for l in range(DEPTH):
        x = x + 0.5 * swiglu_ffn(rmsnorm(x, ffn1_norm[l]), ffn1_w_gate[l], ffn1_w_up[l], ffn1_w_down[l])

        h = rmsnorm(x, mix_norm[l])
        proj = h @ w_in[l]
        u_conv, q, k, v, g_conv, g_attn = jnp.split(proj, split_at, axis=-1)

        y_conv = conformer_conv(u_conv, conv_dw_kernel[l], conv_dw_bias[l],
                                conv_ln_gain[l], conv_ln_bias[l], conv_w_out[l])

        q = q.reshape(B, S, N_GROUPS, HEADS_PER_GROUP, HEAD_DIM)
        k = k.reshape(B, S, N_GROUPS, HEADS_PER_GROUP, HEAD_DIM)
        v = v.reshape(B, S, N_GROUPS, HEADS_PER_GROUP, HEAD_DIM)
        outs, lses = [], []
        for gi, (window, dilation) in enumerate(ATTN_GROUPS):
            o_g, lse_g = dilated_group_attention(
                q[:, :, gi], k[:, :, gi], v[:, :, gi],
                slopes[gi * HEADS_PER_GROUP:(gi + 1) * HEADS_PER_GROUP], window, dilation)
            outs.append(o_g)
            lses.append(lse_g)
        outs = jnp.stack(outs, axis=0)
        lam = jax.nn.softmax(jnp.stack(lses, axis=0), axis=0)
        attn = jnp.sum(lam[..., None].astype(outs.dtype) * outs, axis=0)
        y_attn = attn.reshape(B, S, ATTN_OUT_WIDTH) @ attn_w_out[l]

        mixed = jax.nn.sigmoid(g_conv) * y_conv + jax.nn.sigmoid(g_attn) * y_attn
        x = x + mixed @ w_o[l]

        x = x + 0.5 * swiglu_ffn(rmsnorm(x, ffn2_norm[l]), ffn2_w_gate[l], ffn2_w_up[l], ffn2_w_down[l])
    return rmsnorm(x, final_norm)
```

```python
import functools

import jax
import jax.numpy as jnp
import numpy as np
from jax.experimental import pallas as pl
from jax.experimental.pallas import tpu as pltpu

F32 = jnp.float32
BF16 = jnp.bfloat16

D_MODEL = 1024
D_FF = 2816
CONV_WIDTH = 31
HEAD_DIM = 128
HEADS_PER_GROUP = 4
GROUP_WIDTH = HEADS_PER_GROUP * HEAD_DIM
ATTN_GROUPS = ((128, 1), (512, 4), (2048, 16))
N_GROUPS = len(ATTN_GROUPS)
N_ATTN_HEADS = HEADS_PER_GROUP * N_GROUPS
ATTN_WIDTH = N_ATTN_HEADS * HEAD_DIM
BLOCK = 128
EPS = 1e-6

VMEM_LIMIT_BYTES = 56 * 1024 * 1024

TOKEN_TILE = 512
FF_CHUNK = 256
PROJ_CHUNK = 512
CONV_HALO = 32
CONV_ROWS = 16
SUBLANES = 8


def _alibi_slopes():
    h = np.arange(1, N_ATTN_HEADS + 1, dtype=np.float32)
    return np.power(np.float32(2.0), -8.0 * h / np.float32(N_ATTN_HEADS)).astype(np.float32)


def _rmsnorm(x, g):
    ms = jnp.mean(x * x, axis=-1, keepdims=True)
    return x * jax.lax.rsqrt(ms + EPS) * g


def _const_spec(shape):
    return pl.BlockSpec(shape, lambda *_: (0,) * len(shape), pipeline_mode=pl.Buffered(1))


def _params(n_axes):
    return pltpu.CompilerParams(
        dimension_semantics=("arbitrary",) * n_axes,
        vmem_limit_bytes=VMEM_LIMIT_BYTES)


def _ffn_body(x_ref, g_ref, wg_ref, wu_ref, wd_ref, *rest, final):
    if final:
        fg_ref, o_ref, act_ref = rest
    else:
        o_ref, act_ref = rest
    x = x_ref[...]
    h = _rmsnorm(x, g_ref[...]).astype(BF16)
    for c in range(D_FF // FF_CHUNK):
        sl = slice(c * FF_CHUNK, (c + 1) * FF_CHUNK)
        g = jnp.dot(h, wg_ref[:, sl], preferred_element_type=F32)
        u = jnp.dot(h, wu_ref[:, sl], preferred_element_type=F32)
        act_ref[:, sl] = (g * jax.nn.sigmoid(g) * u).astype(BF16)
    y = jnp.dot(act_ref[...], wd_ref[...], preferred_element_type=F32)
    out = x + 0.5 * y
    if final:
        out = _rmsnorm(out, fg_ref[...])
    o_ref[...] = out


def _ffn(x, norm_g, wg, wu, wd, final_g=None):
    t = x.shape[0]
    final = final_g is not None
    tile = pl.BlockSpec((TOKEN_TILE, D_MODEL), lambda i: (i, 0))
    in_specs = [tile, _const_spec((1, D_MODEL)), _const_spec((D_MODEL, D_FF)),
                _const_spec((D_MODEL, D_FF)), _const_spec((D_FF, D_MODEL))]
    args = [x, norm_g, wg, wu, wd]
    if final:
        in_specs.append(_const_spec((1, D_MODEL)))
        args.append(final_g)
    return pl.pallas_call(
        functools.partial(_ffn_body, final=final),
        grid=(t // TOKEN_TILE,),
        in_specs=in_specs,
        out_specs=tile,
        out_shape=jax.ShapeDtypeStruct((t, D_MODEL), F32),
        scratch_shapes=[pltpu.VMEM((TOKEN_TILE, D_FF), BF16)],
        compiler_params=_params(1),
        name="ffn_final" if final else "ffn",
    )(*args)


def _proj_body(x_ref, g_ref, w_ref, z_ref, q_ref, k_ref, v_ref, sgc_ref, sga_ref):
    h = _rmsnorm(x_ref[...], g_ref[...]).astype(BF16)

    def mm(lo):
        return jnp.dot(h, w_ref[:, lo:lo + PROJ_CHUNK], preferred_element_type=F32)

    for c in range(D_MODEL // PROJ_CHUNK):
        sl = slice(c * PROJ_CHUNK, (c + 1) * PROJ_CHUNK)
        a = mm(c * PROJ_CHUNK)
        b = mm(D_MODEL + c * PROJ_CHUNK)
        z_ref[:, sl] = a * jax.nn.sigmoid(b)
    base = 2 * D_MODEL
    for ref in (q_ref, k_ref, v_ref):
        for c in range(ATTN_WIDTH // PROJ_CHUNK):
            sl = slice(c * PROJ_CHUNK, (c + 1) * PROJ_CHUNK)
            ref[:, sl] = mm(base + c * PROJ_CHUNK).astype(BF16)
        base += ATTN_WIDTH
    for ref in (sgc_ref, sga_ref):
        for c in range(D_MODEL // PROJ_CHUNK):
            sl = slice(c * PROJ_CHUNK, (c + 1) * PROJ_CHUNK)
            ref[:, sl] = jax.nn.sigmoid(mm(base + c * PROJ_CHUNK))
        base += D_MODEL


def _proj(x, norm_g, w_in):
    t = x.shape[0]
    in_width = w_in.shape[1]
    tile = pl.BlockSpec((TOKEN_TILE, D_MODEL), lambda i: (i, 0))
    qkv_tile = pl.BlockSpec((TOKEN_TILE, ATTN_WIDTH), lambda i: (i, 0))
    wide = jax.ShapeDtypeStruct((t, D_MODEL), F32)
    qkv = jax.ShapeDtypeStruct((t, ATTN_WIDTH), BF16)
    return pl.pallas_call(
        _proj_body,
        grid=(t // TOKEN_TILE,),
        in_specs=[tile, _const_spec((1, D_MODEL)), _const_spec((D_MODEL, in_width))],
        out_specs=[tile, qkv_tile, qkv_tile, qkv_tile, tile, tile],
        out_shape=[wide, qkv, qkv, qkv, wide, wide],
        compiler_params=_params(1),
        name="proj",
    )(x, norm_g, w_in)


def _conv_body(z_ref, halo_ref, sgc_ref, dw_ref, db_ref, lg_ref, lb_ref, wo_ref, o_ref,
               zs_ref, acc_ref, *, tiles_per_seq):
    rows = TOKEN_TILE + CONV_HALO
    first = pl.program_id(0) % tiles_per_seq == 0
    halo = jnp.where(first, 0.0, halo_ref[...])
    zs_ref[0, 0:CONV_HALO, :] = halo
    zs_ref[0, CONV_HALO:rows, :] = z_ref[...]
    zz = zs_ref[0]
    for r in range(1, SUBLANES):
        zs_ref[r] = pltpu.roll(zz, rows - r, axis=0)

    lead = CONV_HALO - (CONV_WIDTH - 1)

    def step(i, carry):
        base = pl.multiple_of(i * CONV_ROWS, CONV_ROWS)
        acc = jnp.broadcast_to(db_ref[...], (CONV_ROWS, D_MODEL))
        for k in range(CONV_WIDTH):
            off = lead + k
            win = zs_ref[off % SUBLANES, pl.ds(base + (off // SUBLANES) * SUBLANES, CONV_ROWS), :]
            acc = acc + win * dw_ref[k:k + 1, :]
        acc_ref[pl.ds(base, CONV_ROWS), :] = acc
        return carry

    jax.lax.fori_loop(0, TOKEN_TILE // CONV_ROWS, step, 0)

    c = acc_ref[...]
    mu = jnp.mean(c, axis=-1, keepdims=True)
    d = c - mu
    var = jnp.mean(d * d, axis=-1, keepdims=True)
    y = d * jax.lax.rsqrt(var + EPS) * lg_ref[...] + lb_ref[...]
    y = (y * jax.nn.sigmoid(y)).astype(BF16)
    o_ref[...] = sgc_ref[...] * jnp.dot(y, wo_ref[...], preferred_element_type=F32)


def _conv(z, sgc, dw, db, lg, lb, wo, seq):
    t = z.shape[0]
    tiles_per_seq = seq // TOKEN_TILE
    halo_per_tile = TOKEN_TILE // CONV_HALO
    tile = pl.BlockSpec((TOKEN_TILE, D_MODEL), lambda i: (i, 0))
    halo = pl.BlockSpec((CONV_HALO, D_MODEL),
                        lambda i: (jnp.maximum(i * halo_per_tile - 1, 0), 0))
    return pl.pallas_call(
        functools.partial(_conv_body, tiles_per_seq=tiles_per_seq),
        grid=(t // TOKEN_TILE,),
        in_specs=[tile, halo, tile, _const_spec((CONV_WIDTH, D_MODEL)),
                  _const_spec((1, D_MODEL)), _const_spec((1, D_MODEL)), _const_spec((1, D_MODEL)),
                  _const_spec((D_MODEL, D_MODEL))],
        out_specs=tile,
        out_shape=jax.ShapeDtypeStruct((t, D_MODEL), F32),
        scratch_shapes=[pltpu.VMEM((SUBLANES, TOKEN_TILE + CONV_HALO, D_MODEL), F32),
                        pltpu.VMEM((TOKEN_TILE, D_MODEL), F32)],
        compiler_params=_params(1),
        name="conv",
    )(z, z, sgc, dw, db, lg, lb, wo)


def _attn_body(q_ref, k_ref, v_ref, kp_ref, vp_ref, o_ref, l_ref, *, slopes, dilation, q_tile):
    has_prev = pl.program_id(2) > 0
    qi = jax.lax.broadcasted_iota(jnp.int32, (BLOCK, BLOCK), 0)
    ki = jax.lax.broadcasted_iota(jnp.int32, (BLOCK, BLOCK), 1)
    dist = (qi - ki).astype(F32)
    cur_ok = ki <= qi
    prev_ok = ki >= qi
    lane = jax.lax.broadcasted_iota(jnp.int32, (BLOCK, BLOCK), 1)
    scale = HEAD_DIM ** -0.5
    nt = (((1,), (1,)), ((), ()))
    for n in range(q_tile // BLOCK):
        rows = slice(n * BLOCK, (n + 1) * BLOCK)
        lse_tile = jnp.zeros((BLOCK, BLOCK), F32)
        for h in range(HEADS_PER_GROUP):
            cols = slice(h * HEAD_DIM, (h + 1) * HEAD_DIM)
            q = q_ref[0, rows, cols]
            kc = k_ref[0, rows, cols]
            vc = v_ref[0, rows, cols]
            if n == 0:
                kp = kp_ref[0, :, cols]
                vp = vp_ref[0, :, cols]
                p_ok = jnp.logical_and(prev_ok, has_prev)
            else:
                prows = slice((n - 1) * BLOCK, n * BLOCK)
                kp = k_ref[0, prows, cols]
                vp = v_ref[0, prows, cols]
                p_ok = prev_ok
            neg = -float(slopes[h]) * dilation
            s_c = jax.lax.dot_general(q, kc, nt, preferred_element_type=F32) * scale + neg * dist
            s_p = jax.lax.dot_general(q, kp, nt, preferred_element_type=F32) * scale + neg * (dist + BLOCK)
            s_c = jnp.where(cur_ok, s_c, -jnp.inf)
            s_p = jnp.where(p_ok, s_p, -jnp.inf)
            m = jnp.maximum(jnp.max(s_c, axis=-1, keepdims=True), jnp.max(s_p, axis=-1, keepdims=True))
            p_c = jnp.exp(s_c - m)
            p_p = jnp.exp(s_p - m)
            denom = jnp.sum(p_c, axis=-1, keepdims=True) + jnp.sum(p_p, axis=-1, keepdims=True)
            acc = jnp.dot(p_c.astype(BF16), vc, preferred_element_type=F32)
            acc = acc + jnp.dot(p_p.astype(BF16), vp, preferred_element_type=F32)
            o_ref[0, rows, cols] = acc / denom
            lse = m + jnp.log(denom)
            lane_lo = h * (BLOCK // HEADS_PER_GROUP)
            mine = jnp.logical_and(lane >= lane_lo, lane < lane_lo + BLOCK // HEADS_PER_GROUP)
            lse_tile = jnp.where(mine, lse, lse_tile)
        l_ref[0, rows, :] = lse_tile


def _attn_group(q, k, v, gi, batch, seq):
    _, dilation = ATTN_GROUPS[gi]
    length = seq // dilation
    q_tile = min(length, 512)
    n_tiles = length // q_tile
    blocks_per_tile = q_tile // BLOCK
    slopes = _alibi_slopes()[gi * HEADS_PER_GROUP:(gi + 1) * HEADS_PER_GROUP]
    lane_blocks = ATTN_WIDTH // GROUP_WIDTH

    def view(a):
        return a.reshape(batch, length, dilation * ATTN_WIDTH)

    cur = pl.BlockSpec((1, q_tile, GROUP_WIDTH), lambda b, r, t: (b, t, r * lane_blocks + gi))
    prev = pl.BlockSpec((1, BLOCK, GROUP_WIDTH),
                        lambda b, r, t: (b, jnp.maximum(t * blocks_per_tile - 1, 0), r * lane_blocks + gi))
    out, lse = pl.pallas_call(
        functools.partial(_attn_body, slopes=slopes, dilation=dilation, q_tile=q_tile),
        grid=(batch, dilation, n_tiles),
        in_specs=[cur, cur, cur, prev, prev],
        out_specs=[pl.BlockSpec((1, q_tile, GROUP_WIDTH), lambda b, r, t: (b, t, r)),
                   pl.BlockSpec((1, q_tile, BLOCK), lambda b, r, t: (b, t, r))],
        out_shape=[jax.ShapeDtypeStruct((batch, length, dilation * GROUP_WIDTH), F32),
                   jax.ShapeDtypeStruct((batch, length, dilation * BLOCK), F32)],
        compiler_params=_params(3),
        name=f"attn_d{dilation}",
    )(view(q), view(k), view(v), view(k), view(v))
    return out.reshape(batch * seq, GROUP_WIDTH), lse.reshape(batch * seq, BLOCK)


def _mix_body(x_ref, cp_ref, sga_ref, o0_ref, o1_ref, o2_ref, l0_ref, l1_ref, l2_ref,
              wa_ref, wo_ref, out_ref, attn_ref):
    l0, l1, l2 = l0_ref[...], l1_ref[...], l2_ref[...]
    mx = jnp.maximum(jnp.maximum(l0, l1), l2)
    e0, e1, e2 = jnp.exp(l0 - mx), jnp.exp(l1 - mx), jnp.exp(l2 - mx)
    tot = e0 + e1 + e2
    lanes_per_head = BLOCK // HEADS_PER_GROUP
    for h in range(HEADS_PER_GROUP):
        cols = slice(h * HEAD_DIM, (h + 1) * HEAD_DIM)
        pick = slice(h * lanes_per_head, h * lanes_per_head + 1)
        acc = (e0[:, pick] / tot[:, pick]) * o0_ref[:, cols]
        acc = acc + (e1[:, pick] / tot[:, pick]) * o1_ref[:, cols]
        acc = acc + (e2[:, pick] / tot[:, pick]) * o2_ref[:, cols]
        attn_ref[:, cols] = acc.astype(BF16)
    y_attn = jnp.dot(attn_ref[...], wa_ref[...], preferred_element_type=F32)
    mixed = (cp_ref[...] + sga_ref[...] * y_attn).astype(BF16)
    out_ref[...] = x_ref[...] + jnp.dot(mixed, wo_ref[...], preferred_element_type=F32)


def _mix(x, cpart, sga, outs, lses, wa, wo):
    t = x.shape[0]
    tile = pl.BlockSpec((TOKEN_TILE, D_MODEL), lambda i: (i, 0))
    o_tile = pl.BlockSpec((TOKEN_TILE, GROUP_WIDTH), lambda i: (i, 0))
    l_tile = pl.BlockSpec((TOKEN_TILE, BLOCK), lambda i: (i, 0))
    return pl.pallas_call(
        _mix_body,
        grid=(t // TOKEN_TILE,),
        in_specs=[tile, tile, tile, o_tile, o_tile, o_tile, l_tile, l_tile, l_tile,
                  _const_spec((GROUP_WIDTH, D_MODEL)), _const_spec((D_MODEL, D_MODEL))],
        out_specs=tile,
        out_shape=jax.ShapeDtypeStruct((t, D_MODEL), F32),
        scratch_shapes=[pltpu.VMEM((TOKEN_TILE, GROUP_WIDTH), BF16)],
        compiler_params=_params(1),
        name="mix",
    )(x, cpart, sga, *outs, *lses, wa, wo)


def kernel(x, ffn1_norm, ffn1_w_gate, ffn1_w_up, ffn1_w_down, mix_norm, w_in, conv_dw_kernel, conv_dw_bias, conv_ln_gain, conv_ln_bias, conv_w_out, attn_w_out, w_o, ffn2_norm, ffn2_w_gate, ffn2_w_up, ffn2_w_down, final_norm):
    batch, seq, _ = x.shape
    depth = ffn1_norm.shape[0]
    assert seq % (ATTN_GROUPS[-1][1] * BLOCK) == 0 and seq % TOKEN_TILE == 0
    xt = x.reshape(batch * seq, D_MODEL)
    for l in range(depth):
        last = l == depth - 1
        xt = _ffn(xt, ffn1_norm[l][None], ffn1_w_gate[l].astype(BF16), ffn1_w_up[l].astype(BF16),
                  ffn1_w_down[l].astype(BF16))
        z, q, k, v, sgc, sga = _proj(xt, mix_norm[l][None], w_in[l].astype(BF16))
        cpart = _conv(z, sgc, conv_dw_kernel[l], conv_dw_bias[l][None], conv_ln_gain[l][None],
                      conv_ln_bias[l][None], conv_w_out[l].astype(BF16), seq)
        outs, lses = [], []
        for gi in range(N_GROUPS):
            o_g, l_g = _attn_group(q, k, v, gi, batch, seq)
            outs.append(o_g)
            lses.append(l_g)
        xt = _mix(xt, cpart, sga, outs, lses, attn_w_out[l].astype(BF16), w_o[l].astype(BF16))
        xt = _ffn(xt, ffn2_norm[l][None], ffn2_w_gate[l].astype(BF16), ffn2_w_up[l].astype(BF16),
                  ffn2_w_down[l].astype(BF16), final_norm[None] if last else None)
    if depth == 0:
        raise ValueError("depth must be positive")
    return xt.reshape(batch, seq, D_MODEL)
```

```python
import functools

import jax
import jax.numpy as jnp
import numpy as np
from jax.experimental import pallas as pl
from jax.experimental.pallas import tpu as pltpu

F32 = jnp.float32
BF16 = jnp.bfloat16

D_MODEL = 1024
D_FF = 2816
CONV_WIDTH = 31
HEAD_DIM = 128
HEADS_PER_GROUP = 4
GROUP_WIDTH = HEADS_PER_GROUP * HEAD_DIM
ATTN_GROUPS = ((128, 1), (512, 4), (2048, 16))
N_GROUPS = len(ATTN_GROUPS)
N_ATTN_HEADS = HEADS_PER_GROUP * N_GROUPS
ATTN_WIDTH = N_ATTN_HEADS * HEAD_DIM
QKV_WIDTH = 3 * GROUP_WIDTH
BLOCK = 128
EPS = 1e-6

LANES = 128
SUBLANES = 8
LANE_BLOCKS = D_MODEL // LANES
VMEM_LIMIT_BYTES = 56 * 1024 * 1024

TOKEN_TILE = 512
FF_CHUNK = 256
PROJ_CHUNK = 512
CONV_HALO = 32
CONV_ROWS = 64
ATTN_CHUNK = ATTN_GROUPS[-1][1] * BLOCK
LSE_LANES = LANES // HEADS_PER_GROUP


def _alibi_slopes():
    h = np.arange(1, N_ATTN_HEADS + 1, dtype=np.float32)
    return np.power(np.float32(2.0), -8.0 * h / np.float32(N_ATTN_HEADS)).astype(np.float32)


def _rmsnorm(x, g):
    ms = jnp.mean(x * x, axis=-1, keepdims=True)
    return x * jax.lax.rsqrt(ms + EPS) * g


def _const_spec(shape):
    return pl.BlockSpec(shape, lambda *_: (0,) * len(shape), pipeline_mode=pl.Buffered(1))


def _params(n_axes):
    return pltpu.CompilerParams(
        dimension_semantics=("arbitrary",) * n_axes,
        vmem_limit_bytes=VMEM_LIMIT_BYTES)


def _ffn_body(x_ref, g_ref, wg_ref, wu_ref, wd_ref, *rest, final):
    if final:
        fg_ref, o_ref, act_ref = rest
    else:
        o_ref, act_ref = rest
    x = x_ref[...]
    h = _rmsnorm(x, g_ref[...]).astype(BF16)
    for c in range(D_FF // FF_CHUNK):
        sl = slice(c * FF_CHUNK, (c + 1) * FF_CHUNK)
        g = jnp.dot(h, wg_ref[:, sl], preferred_element_type=F32)
        u = jnp.dot(h, wu_ref[:, sl], preferred_element_type=F32)
        act_ref[:, sl] = (g * jax.nn.sigmoid(g) * u).astype(BF16)
    y = jnp.dot(act_ref[...], wd_ref[...], preferred_element_type=F32)
    out = x + 0.5 * y
    if final:
        out = _rmsnorm(out, fg_ref[...])
    o_ref[...] = out


def _ffn(x, norm_g, wg, wu, wd, final_g=None):
    t = x.shape[0]
    final = final_g is not None
    tile = pl.BlockSpec((TOKEN_TILE, D_MODEL), lambda i: (i, 0))
    in_specs = [tile, _const_spec((1, D_MODEL)), _const_spec((D_MODEL, D_FF)),
                _const_spec((D_MODEL, D_FF)), _const_spec((D_FF, D_MODEL))]
    args = [x, norm_g, wg, wu, wd]
    if final:
        in_specs.append(_const_spec((1, D_MODEL)))
        args.append(final_g)
    return pl.pallas_call(
        functools.partial(_ffn_body, final=final),
        grid=(t // TOKEN_TILE,),
        in_specs=in_specs,
        out_specs=tile,
        out_shape=jax.ShapeDtypeStruct((t, D_MODEL), F32),
        scratch_shapes=[pltpu.VMEM((TOKEN_TILE, D_FF), BF16)],
        compiler_params=_params(1),
        name="ffn_final" if final else "ffn",
    )(*args)


def _proj_body(x_ref, g_ref, w_ref, z_ref, qkv0_ref, qkv1_ref, qkv2_ref, sgc_ref, sga_ref,
               xs_ref, hp_ref):
    xn = _rmsnorm(x_ref[...], g_ref[...])
    h = xn.astype(BF16)

    def mm(lhs, lo, n=PROJ_CHUNK):
        return jnp.dot(lhs, w_ref[:, lo:lo + n], preferred_element_type=F32)

    for c in range(D_MODEL // PROJ_CHUNK):
        sl = slice(c * PROJ_CHUNK, (c + 1) * PROJ_CHUNK)
        a = mm(h, c * PROJ_CHUNK)
        b = mm(h, D_MODEL + c * PROJ_CHUNK)
        z_ref[:, sl] = a * jax.nn.sigmoid(b)
    gate_base = 2 * D_MODEL + 3 * ATTN_WIDTH
    for j, ref in enumerate((sgc_ref, sga_ref)):
        for c in range(D_MODEL // PROJ_CHUNK):
            sl = slice(c * PROJ_CHUNK, (c + 1) * PROJ_CHUNK)
            ref[:, sl] = jax.nn.sigmoid(mm(h, gate_base + j * D_MODEL + c * PROJ_CHUNK))

    for c in range(LANE_BLOCKS):
        xs_ref[c] = xn[:, c * LANES:(c + 1) * LANES]

    for gi, ref in enumerate((qkv0_ref, qkv1_ref, qkv2_ref)):
        dilation = ATTN_GROUPS[gi][1]
        per = TOKEN_TILE // dilation
        if dilation == 1:
            lhs = h
        else:
            for r in range(dilation):
                for c in range(LANE_BLOCKS):
                    rows = xs_ref[c, pl.ds(r, per, stride=dilation), :]
                    hp_ref[r * per:(r + 1) * per, c * LANES:(c + 1) * LANES] = rows.astype(BF16)
            lhs = hp_ref[...]
        for part in range(3):
            res = mm(lhs, 2 * D_MODEL + part * ATTN_WIDTH + gi * GROUP_WIDTH, GROUP_WIDTH).astype(BF16)
            for r in range(dilation):
                ref[r, :, part * GROUP_WIDTH:(part + 1) * GROUP_WIDTH] = res[r * per:(r + 1) * per]


def _proj(x, norm_g, w_in, batch, seq):
    in_width = w_in.shape[1]
    tiles = seq // TOKEN_TILE
    x3 = x.reshape(batch, seq, D_MODEL)
    tile = pl.BlockSpec((None, TOKEN_TILE, D_MODEL), lambda b, j: (b, j, 0))
    wide = jax.ShapeDtypeStruct((batch, seq, D_MODEL), F32)
    qkv_specs, qkv_shapes = [], []
    for _, dilation in ATTN_GROUPS:
        qkv_specs.append(pl.BlockSpec((None, dilation, TOKEN_TILE // dilation, QKV_WIDTH),
                                      lambda b, j: (b, 0, j, 0)))
        qkv_shapes.append(jax.ShapeDtypeStruct((batch, dilation, seq // dilation, QKV_WIDTH), BF16))
    return pl.pallas_call(
        _proj_body,
        grid=(batch, tiles),
        in_specs=[tile, _const_spec((1, D_MODEL)), _const_spec((D_MODEL, in_width))],
        out_specs=[tile, *qkv_specs, tile, tile],
        out_shape=[wide, *qkv_shapes, wide, wide],
        scratch_shapes=[pltpu.VMEM((LANE_BLOCKS, TOKEN_TILE, LANES), F32),
                        pltpu.VMEM((TOKEN_TILE, D_MODEL), BF16)],
        compiler_params=_params(2),
        name="proj",
    )(x3, norm_g, w_in)


def _conv_body(z_ref, halo_ref, sgc_ref, dw_ref, db_ref, lg_ref, lb_ref, wo_ref, o_ref,
               zs_ref, wb_ref, acc_ref, *, tiles_per_seq):
    rows = TOKEN_TILE + CONV_HALO

    @pl.when(pl.program_id(0) == 0)
    def _():
        for k in range(CONV_WIDTH):
            wb_ref[k] = jnp.broadcast_to(dw_ref[k:k + 1, :], (SUBLANES, D_MODEL))

    first = pl.program_id(0) % tiles_per_seq == 0
    for c in range(LANE_BLOCKS):
        cols = slice(c * LANES, (c + 1) * LANES)
        zs_ref[0, c, 0:CONV_HALO, :] = jnp.where(first, 0.0, halo_ref[:, cols])
        zs_ref[0, c, CONV_HALO:rows, :] = z_ref[:, cols]
        zz = zs_ref[0, c]
        for r in range(1, SUBLANES):
            zs_ref[r, c] = pltpu.roll(zz, rows - r, axis=0)

    lead = CONV_HALO - (CONV_WIDTH - 1)

    def step(i, carry):
        base = pl.multiple_of(i * CONV_ROWS, CONV_ROWS)
        for c in range(LANE_BLOCKS):
            cols = slice(c * LANES, (c + 1) * LANES)
            acc = jnp.broadcast_to(db_ref[:, cols], (CONV_ROWS, LANES))
            for k in range(CONV_WIDTH):
                off = lead + k
                win = zs_ref[off % SUBLANES, c, pl.ds(base + (off // SUBLANES) * SUBLANES, CONV_ROWS), :]
                tap = jnp.concatenate([wb_ref[k, :, cols]] * (CONV_ROWS // SUBLANES), axis=0)
                acc = acc + win * tap
            acc_ref[pl.ds(base, CONV_ROWS), cols] = acc
        return carry

    jax.lax.fori_loop(0, TOKEN_TILE // CONV_ROWS, step, 0)

    c = acc_ref[...]
    mu = jnp.mean(c, axis=-1, keepdims=True)
    d = c - mu
    var = jnp.mean(d * d, axis=-1, keepdims=True)
    y = d * jax.lax.rsqrt(var + EPS) * lg_ref[...] + lb_ref[...]
    y = (y * jax.nn.sigmoid(y)).astype(BF16)
    o_ref[...] = sgc_ref[...] * jnp.dot(y, wo_ref[...], preferred_element_type=F32)


def _conv(z, sgc, dw, db, lg, lb, wo, seq):
    t = z.shape[0]
    tiles_per_seq = seq // TOKEN_TILE
    halo_per_tile = TOKEN_TILE // CONV_HALO
    tile = pl.BlockSpec((TOKEN_TILE, D_MODEL), lambda i: (i, 0))
    halo = pl.BlockSpec((CONV_HALO, D_MODEL),
                        lambda i: (jnp.maximum(i * halo_per_tile - 1, 0), 0))
    return pl.pallas_call(
        functools.partial(_conv_body, tiles_per_seq=tiles_per_seq),
        grid=(t // TOKEN_TILE,),
        in_specs=[tile, halo, tile, _const_spec((CONV_WIDTH, D_MODEL)),
                  _const_spec((1, D_MODEL)), _const_spec((1, D_MODEL)), _const_spec((1, D_MODEL)),
                  _const_spec((D_MODEL, D_MODEL))],
        out_specs=tile,
        out_shape=jax.ShapeDtypeStruct((t, D_MODEL), F32),
        scratch_shapes=[pltpu.VMEM((SUBLANES, LANE_BLOCKS, TOKEN_TILE + CONV_HALO, LANES), F32),
                        pltpu.VMEM((CONV_WIDTH, SUBLANES, D_MODEL), F32),
                        pltpu.VMEM((TOKEN_TILE, D_MODEL), F32)],
        compiler_params=_params(1),
        name="conv",
    )(z, z, sgc, dw, db, lg, lb, wo)


def _attn_body(q_ref, k_ref, v_ref, kp_ref, vp_ref, o_ref, l_ref, *, slopes, dilation):
    per = ATTN_CHUNK // dilation
    chunk_has_prev = pl.program_id(1) > 0
    qi = jax.lax.broadcasted_iota(jnp.int32, (BLOCK, 2 * BLOCK), 0)
    kj = jax.lax.broadcasted_iota(jnp.int32, (BLOCK, 2 * BLOCK), 1)
    steps = BLOCK + qi - kj
    in_band = jnp.logical_and(steps >= 0, steps <= BLOCK)
    own_block = kj >= BLOCK
    lane = jax.lax.broadcasted_iota(jnp.int32, (BLOCK, LANES), 1)
    scale = HEAD_DIM ** -0.5
    nt = (((1,), (1,)), ((), ()))
    biases = []
    for h in range(HEADS_PER_GROUP):
        alibi = (-float(slopes[h]) * dilation) * steps.astype(F32)
        biases.append(jnp.where(in_band, alibi, -jnp.inf))
    for r in range(dilation):
        for n in range(per // BLOCK):
            rows = slice(n * BLOCK, (n + 1) * BLOCK)
            lse_tile = jnp.zeros((BLOCK, LANES), F32)
            for h in range(HEADS_PER_GROUP):
                cols = slice(h * HEAD_DIM, (h + 1) * HEAD_DIM)
                q = q_ref[r, rows, cols]
                if n == 0:
                    kk = jnp.concatenate([kp_ref[r, :, cols], k_ref[r, rows, cols]], axis=0)
                    vv = jnp.concatenate([vp_ref[r, :, cols], v_ref[r, rows, cols]], axis=0)
                else:
                    both = slice((n - 1) * BLOCK, (n + 1) * BLOCK)
                    kk = k_ref[r, both, cols]
                    vv = v_ref[r, both, cols]
                s = jax.lax.dot_general(q, kk, nt, preferred_element_type=F32) * scale + biases[h]
                if n == 0:
                    s = jnp.where(jnp.logical_or(own_block, chunk_has_prev), s, -jnp.inf)
                m = jnp.max(s, axis=-1, keepdims=True)
                p = jnp.exp(s - m)
                denom = jnp.sum(p, axis=-1, keepdims=True)
                acc = jnp.dot(p.astype(BF16), vv, preferred_element_type=F32)
                start = (n * BLOCK) * dilation + r
                o_ref[h, pl.ds(start, BLOCK, stride=dilation), :] = acc / denom
                lse = m + jnp.log(denom)
                mine = jnp.logical_and(lane >= h * LSE_LANES, lane < (h + 1) * LSE_LANES)
                lse_tile = jnp.where(mine, lse, lse_tile)
            l_ref[pl.ds((n * BLOCK) * dilation + r, BLOCK, stride=dilation), :] = lse_tile


def _attn_group(qkv, gi, batch, seq):
    _, dilation = ATTN_GROUPS[gi]
    per = ATTN_CHUNK // dilation
    blocks_per_chunk = per // BLOCK
    slopes = _alibi_slopes()[gi * HEADS_PER_GROUP:(gi + 1) * HEADS_PER_GROUP]

    def cur(part):
        return pl.BlockSpec((None, dilation, per, GROUP_WIDTH), lambda b, c: (b, 0, c, part))

    def prev(part):
        return pl.BlockSpec((None, dilation, BLOCK, GROUP_WIDTH),
                            lambda b, c: (b, 0, jnp.maximum(c * blocks_per_chunk - 1, 0), part))

    return pl.pallas_call(
        functools.partial(_attn_body, slopes=slopes, dilation=dilation),
        grid=(batch, seq // ATTN_CHUNK),
        in_specs=[cur(0), cur(1), cur(2), prev(1), prev(2)],
        out_specs=[pl.BlockSpec((None, HEADS_PER_GROUP, ATTN_CHUNK, HEAD_DIM), lambda b, c: (b, 0, c, 0)),
                   pl.BlockSpec((None, ATTN_CHUNK, LANES), lambda b, c: (b, c, 0))],
        out_shape=[jax.ShapeDtypeStruct((batch, HEADS_PER_GROUP, seq, HEAD_DIM), F32),
                   jax.ShapeDtypeStruct((batch, seq, LANES), F32)],
        compiler_params=_params(2),
        name=f"attn_d{dilation}",
    )(qkv, qkv, qkv, qkv, qkv)


def _mix_body(x_ref, cp_ref, sga_ref, o0_ref, o1_ref, o2_ref, l0_ref, l1_ref, l2_ref,
              wa_ref, wo_ref, out_ref, attn_ref):
    l0, l1, l2 = l0_ref[...], l1_ref[...], l2_ref[...]
    mx = jnp.maximum(jnp.maximum(l0, l1), l2)
    e0, e1, e2 = jnp.exp(l0 - mx), jnp.exp(l1 - mx), jnp.exp(l2 - mx)
    tot = e0 + e1 + e2
    for h in range(HEADS_PER_GROUP):
        cols = slice(h * HEAD_DIM, (h + 1) * HEAD_DIM)
        pick = slice(h * LSE_LANES, h * LSE_LANES + 1)
        acc = (e0[:, pick] / tot[:, pick]) * o0_ref[h]
        acc = acc + (e1[:, pick] / tot[:, pick]) * o1_ref[h]
        acc = acc + (e2[:, pick] / tot[:, pick]) * o2_ref[h]
        attn_ref[:, cols] = acc.astype(BF16)
    y_attn = jnp.dot(attn_ref[...], wa_ref[...], preferred_element_type=F32)
    mixed = (cp_ref[...] + sga_ref[...] * y_attn).astype(BF16)
    out_ref[...] = x_ref[...] + jnp.dot(mixed, wo_ref[...], preferred_element_type=F32)


def _mix(x, cpart, sga, outs, lses, wa, wo, batch, seq):
    tiles = seq // TOKEN_TILE
    tile = pl.BlockSpec((None, TOKEN_TILE, D_MODEL), lambda b, j: (b, j, 0))
    o_tile = pl.BlockSpec((None, HEADS_PER_GROUP, TOKEN_TILE, HEAD_DIM), lambda b, j: (b, 0, j, 0))
    l_tile = pl.BlockSpec((None, TOKEN_TILE, LANES), lambda b, j: (b, j, 0))
    shape3 = (batch, seq, D_MODEL)
    return pl.pallas_call(
        _mix_body,
        grid=(batch, tiles),
        in_specs=[tile, tile, tile, o_tile, o_tile, o_tile, l_tile, l_tile, l_tile,
                  _const_spec((GROUP_WIDTH, D_MODEL)), _const_spec((D_MODEL, D_MODEL))],
        out_specs=tile,
        out_shape=jax.ShapeDtypeStruct(shape3, F32),
        scratch_shapes=[pltpu.VMEM((TOKEN_TILE, GROUP_WIDTH), BF16)],
        compiler_params=_params(2),
        name="mix",
    )(x.reshape(shape3), cpart.reshape(shape3), sga, *outs, *lses, wa, wo)


def kernel(x, ffn1_norm, ffn1_w_gate, ffn1_w_up, ffn1_w_down, mix_norm, w_in, conv_dw_kernel, conv_dw_bias, conv_ln_gain, conv_ln_bias, conv_w_out, attn_w_out, w_o, ffn2_norm, ffn2_w_gate, ffn2_w_up, ffn2_w_down, final_norm):
    batch, seq, _ = x.shape
    depth = ffn1_norm.shape[0]
    assert seq % ATTN_CHUNK == 0 and seq % TOKEN_TILE == 0
    tokens = batch * seq
    xt = x.reshape(tokens, D_MODEL)
    for l in range(depth):
        last = l == depth - 1
        xt = _ffn(xt, ffn1_norm[l][None], ffn1_w_gate[l].astype(BF16), ffn1_w_up[l].astype(BF16),
                  ffn1_w_down[l].astype(BF16))
        z, qkv0, qkv1, qkv2, sgc, sga = _proj(xt, mix_norm[l][None], w_in[l].astype(BF16), batch, seq)
        cpart = _conv(z.reshape(tokens, D_MODEL), sgc.reshape(tokens, D_MODEL), conv_dw_kernel[l],
                      conv_dw_bias[l][None], conv_ln_gain[l][None], conv_ln_bias[l][None],
                      conv_w_out[l].astype(BF16), seq)
        outs, lses = [], []
        for gi, qkv in enumerate((qkv0, qkv1, qkv2)):
            o_g, l_g = _attn_group(qkv, gi, batch, seq)
            outs.append(o_g)
            lses.append(l_g)
        xt = _mix(xt, cpart, sga, outs, lses, attn_w_out[l].astype(BF16), w_o[l].astype(BF16), batch, seq)
        xt = xt.reshape(tokens, D_MODEL)
        xt = _ffn(xt, ffn2_norm[l][None], ffn2_w_gate[l].astype(BF16), ffn2_w_up[l].astype(BF16),
                  ffn2_w_down[l].astype(BF16), final_norm[None] if last else None)
    return xt.reshape(batch, seq, D_MODEL)
```

```python
import functools

import jax
import jax.numpy as jnp
import numpy as np
from jax.experimental import pallas as pl
from jax.experimental.pallas import tpu as pltpu

F32 = jnp.float32
BF16 = jnp.bfloat16

D_MODEL = 1024
D_FF = 2816
CONV_WIDTH = 31
HEAD_DIM = 128
HEADS_PER_GROUP = 4
GROUP_WIDTH = HEADS_PER_GROUP * HEAD_DIM
ATTN_GROUPS = ((128, 1), (512, 4), (2048, 16))
N_GROUPS = len(ATTN_GROUPS)
N_ATTN_HEADS = HEADS_PER_GROUP * N_GROUPS
ATTN_WIDTH = N_ATTN_HEADS * HEAD_DIM
QKV_WIDTH = 3 * GROUP_WIDTH
BLOCK = 128
EPS = 1e-6

LANES = 128
SUBLANES = 8
LANE_BLOCKS = D_MODEL // LANES
VMEM_LIMIT_BYTES = 56 * 1024 * 1024

TOKEN_TILE = 512
TAIL_TILE = 256
FF_CHUNK = 256
PROJ_CHUNK = 512
CONV_HALO = 32
CONV_ROWS = 64
CONV_SHARES = (5,) * 8 + (4,) * 6 + (0,)
ATTN_CHUNK = ATTN_GROUPS[-1][1] * BLOCK
LSE_LANES = LANES // HEADS_PER_GROUP


def _alibi_slopes():
    h = np.arange(1, N_ATTN_HEADS + 1, dtype=np.float32)
    return np.power(np.float32(2.0), -8.0 * h / np.float32(N_ATTN_HEADS)).astype(np.float32)


def _rmsnorm(x, g):
    ms = jnp.mean(x * x, axis=-1, keepdims=True)
    return x * jax.lax.rsqrt(ms + EPS) * g


def _const_spec(shape):
    return pl.BlockSpec(shape, lambda *_: (0,) * len(shape), pipeline_mode=pl.Buffered(1))


def _params(n_axes):
    return pltpu.CompilerParams(
        dimension_semantics=("arbitrary",) * n_axes,
        vmem_limit_bytes=VMEM_LIMIT_BYTES)


def _swiglu_chunk(h, wg_ref, wu_ref, act_ref, c):
    sl = slice(c * FF_CHUNK, (c + 1) * FF_CHUNK)
    g = jnp.dot(h, wg_ref[:, sl], preferred_element_type=F32)
    u = jnp.dot(h, wu_ref[:, sl], preferred_element_type=F32)
    act_ref[:, sl] = (g * jax.nn.sigmoid(g) * u).astype(BF16)


def _ffn_body(x_ref, g_ref, wg_ref, wu_ref, wd_ref, o_ref, act_ref):
    x = x_ref[...]
    h = _rmsnorm(x, g_ref[...]).astype(BF16)
    for c in range(D_FF // FF_CHUNK):
        _swiglu_chunk(h, wg_ref, wu_ref, act_ref, c)
    y = jnp.dot(act_ref[...], wd_ref[...], preferred_element_type=F32)
    o_ref[...] = x + 0.5 * y


def _ffn(x, norm_g, wg, wu, wd):
    t = x.shape[0]
    tile = pl.BlockSpec((TOKEN_TILE, D_MODEL), lambda i: (i, 0))
    return pl.pallas_call(
        _ffn_body,
        grid=(t // TOKEN_TILE,),
        in_specs=[tile, _const_spec((1, D_MODEL)), _const_spec((D_MODEL, D_FF)),
                  _const_spec((D_MODEL, D_FF)), _const_spec((D_FF, D_MODEL))],
        out_specs=tile,
        out_shape=jax.ShapeDtypeStruct((t, D_MODEL), F32),
        scratch_shapes=[pltpu.VMEM((TOKEN_TILE, D_FF), BF16)],
        compiler_params=_params(1),
        name="ffn",
    )(x, norm_g, wg, wu, wd)


def _anchored_zero(v):
    u = pltpu.bitcast(v, jnp.uint32)
    u = jax.lax.shift_right_logical(jax.lax.shift_right_logical(u, jnp.uint32(16)), jnp.uint32(16))
    return pltpu.bitcast(u, F32)


def _conv_piece(zz_ref, rc, cb, wb_ref, db_ref, anchor):
    cols = slice(cb * LANES, (cb + 1) * LANES)
    span = CONV_HALO + CONV_ROWS
    window = zz_ref[rc * CONV_ROWS:rc * CONV_ROWS + span, cols]
    window = window + jnp.concatenate([anchor] * (span // SUBLANES), axis=0)
    lead = CONV_HALO - (CONV_WIDTH - 1)
    acc = jnp.broadcast_to(db_ref[:, cols], (CONV_ROWS, LANES))
    for r in range(SUBLANES):
        shifted = window if r == 0 else pltpu.roll(window, span - r, axis=0)
        for a in range((span - CONV_ROWS) // SUBLANES + 1):
            k = a * SUBLANES + r - lead
            if 0 <= k < CONV_WIDTH:
                tap = jnp.concatenate([wb_ref[k, :, cols]] * (CONV_ROWS // SUBLANES), axis=0)
                acc = acc + shifted[a * SUBLANES:a * SUBLANES + CONV_ROWS] * tap
    return acc


def _proj_body(x_ref, g_ref, w_ref, dw_ref, db_ref, c_ref, qkv0_ref, qkv1_ref, qkv2_ref, sgc_ref, sga_ref,
               h_ref, xs_ref, hp_ref, zz_ref, wb_ref):
    first_step = jnp.logical_and(pl.program_id(0) == 0, pl.program_id(1) == 0)

    @pl.when(first_step)
    def _():
        for k in range(CONV_WIDTH):
            wb_ref[k] = jnp.broadcast_to(dw_ref[k:k + 1, :], (SUBLANES, D_MODEL))

    @pl.when(pl.program_id(1) == 0)
    def _():
        zz_ref[0:CONV_HALO, :] = jnp.zeros((CONV_HALO, D_MODEL), F32)

    xn = _rmsnorm(x_ref[...], g_ref[...])
    h_ref[...] = xn.astype(BF16)
    for c in range(LANE_BLOCKS):
        xs_ref[c] = xn[:, c * LANES:(c + 1) * LANES]

    lanes_per_chunk = PROJ_CHUNK // LANES
    pieces = [(rc, cb) for half in range(D_MODEL // PROJ_CHUNK)
              for rc in range(TOKEN_TILE // CONV_ROWS)
              for cb in range(half * lanes_per_chunk, (half + 1) * lanes_per_chunk)]
    state = {"next_piece": 0, "chain": None, "done": [None]}

    def dot(lhs_ref, lo, n=PROJ_CHUNK):
        done = state["done"]
        if len(done) >= 2 and done[-2] is not None:
            rows = lhs_ref[0:2 * SUBLANES, 0:LANES].astype(F32)
            zero = _anchored_zero(done[-2])
            lhs_ref[0:2 * SUBLANES, 0:LANES] = (rows + jnp.concatenate([zero, zero], axis=0)).astype(BF16)
        return jnp.dot(lhs_ref[...], w_ref[:, lo:lo + n], preferred_element_type=F32)

    def conv_share(result, count):
        anchor = _anchored_zero(result[0:SUBLANES, 0:LANES])
        for rc, cb in pieces[state["next_piece"]:state["next_piece"] + count]:
            if state["chain"] is not None:
                anchor = anchor + _anchored_zero(state["chain"])
            acc = _conv_piece(zz_ref, rc, cb, wb_ref, db_ref, anchor)
            c_ref[rc * CONV_ROWS:(rc + 1) * CONV_ROWS, cb * LANES:(cb + 1) * LANES] = acc
            state["chain"] = acc[0:SUBLANES]
            anchor = jnp.zeros((SUBLANES, LANES), F32)
        state["next_piece"] += count
        state["done"].append(state["chain"])

    n_half = D_MODEL // PROJ_CHUNK
    shares = iter(CONV_SHARES)
    for c in range(n_half):
        sl = slice(c * PROJ_CHUNK, (c + 1) * PROJ_CHUNK)
        a = dot(h_ref, c * PROJ_CHUNK)
        if c > 0:
            conv_share(a, next(shares))
        b = dot(h_ref, D_MODEL + c * PROJ_CHUNK)
        z = a * jax.nn.sigmoid(b)
        zz_ref[CONV_HALO:, sl] = z
        if c > 0:
            conv_share(z, next(shares))

    gate_base = 2 * D_MODEL + 3 * ATTN_WIDTH
    for j, ref in enumerate((sgc_ref, sga_ref)):
        for c in range(n_half):
            sl = slice(c * PROJ_CHUNK, (c + 1) * PROJ_CHUNK)
            gate = jax.nn.sigmoid(dot(h_ref, gate_base + j * D_MODEL + c * PROJ_CHUNK))
            ref[:, sl] = gate
            conv_share(gate, next(shares))

    for gi, ref in enumerate((qkv0_ref, qkv1_ref, qkv2_ref)):
        dilation = ATTN_GROUPS[gi][1]
        per = TOKEN_TILE // dilation
        if dilation == 1:
            lhs_ref = h_ref
        else:
            for r in range(dilation):
                for c in range(LANE_BLOCKS):
                    rows = xs_ref[c, pl.ds(r, per, stride=dilation), :]
                    hp_ref[r * per:(r + 1) * per, c * LANES:(c + 1) * LANES] = rows.astype(BF16)
            lhs_ref = hp_ref
        for part in range(3):
            res32 = dot(lhs_ref, 2 * D_MODEL + part * ATTN_WIDTH + gi * GROUP_WIDTH, GROUP_WIDTH)
            res = res32.astype(BF16)
            for r in range(dilation):
                ref[r, :, part * GROUP_WIDTH:(part + 1) * GROUP_WIDTH] = res[r * per:(r + 1) * per]
            conv_share(res32, next(shares))
    assert state["next_piece"] == len(pieces)

    zz_ref[0:CONV_HALO, :] = zz_ref[TOKEN_TILE:TOKEN_TILE + CONV_HALO, :]


def _proj(x, norm_g, w_in, dw, db, batch, seq):
    in_width = w_in.shape[1]
    tiles = seq // TOKEN_TILE
    x3 = x.reshape(batch, seq, D_MODEL)
    tile = pl.BlockSpec((None, TOKEN_TILE, D_MODEL), lambda b, j: (b, j, 0))
    wide = jax.ShapeDtypeStruct((batch, seq, D_MODEL), F32)
    qkv_specs, qkv_shapes = [], []
    for _, dilation in ATTN_GROUPS:
        qkv_specs.append(pl.BlockSpec((None, dilation, TOKEN_TILE // dilation, QKV_WIDTH),
                                      lambda b, j: (b, 0, j, 0)))
        qkv_shapes.append(jax.ShapeDtypeStruct((batch, dilation, seq // dilation, QKV_WIDTH), BF16))
    return pl.pallas_call(
        _proj_body,
        grid=(batch, tiles),
        in_specs=[tile, _const_spec((1, D_MODEL)), _const_spec((D_MODEL, in_width)),
                  _const_spec((CONV_WIDTH, D_MODEL)), _const_spec((1, D_MODEL))],
        out_specs=[tile, *qkv_specs, tile, tile],
        out_shape=[wide, *qkv_shapes, wide, wide],
        scratch_shapes=[pltpu.VMEM((TOKEN_TILE, D_MODEL), BF16),
                        pltpu.VMEM((LANE_BLOCKS, TOKEN_TILE, LANES), F32),
                        pltpu.VMEM((TOKEN_TILE, D_MODEL), BF16),
                        pltpu.VMEM((CONV_HALO + TOKEN_TILE, D_MODEL), F32),
                        pltpu.VMEM((CONV_WIDTH, SUBLANES, D_MODEL), F32)],
        compiler_params=_params(2),
        name="proj",
    )(x3, norm_g, w_in, dw, db)


def _attn_body(q_ref, k_ref, v_ref, kp_ref, vp_ref, o_ref, l_ref, *, slopes, dilation):
    per = ATTN_CHUNK // dilation
    chunk_has_prev = pl.program_id(1) > 0
    qi = jax.lax.broadcasted_iota(jnp.int32, (BLOCK, 2 * BLOCK), 0)
    kj = jax.lax.broadcasted_iota(jnp.int32, (BLOCK, 2 * BLOCK), 1)
    steps = BLOCK + qi - kj
    in_band = jnp.logical_and(steps >= 0, steps <= BLOCK)
    own_block = kj >= BLOCK
    lane = jax.lax.broadcasted_iota(jnp.int32, (BLOCK, LANES), 1)
    scale = HEAD_DIM ** -0.5
    nt = (((1,), (1,)), ((), ()))
    biases = []
    for h in range(HEADS_PER_GROUP):
        alibi = (-float(slopes[h]) * dilation) * steps.astype(F32)
        biases.append(jnp.where(in_band, alibi, -jnp.inf))
    for r in range(dilation):
        for n in range(per // BLOCK):
            rows = slice(n * BLOCK, (n + 1) * BLOCK)
            lse_tile = jnp.zeros((BLOCK, LANES), F32)
            for h in range(HEADS_PER_GROUP):
                cols = slice(h * HEAD_DIM, (h + 1) * HEAD_DIM)
                q = q_ref[r, rows, cols]
                if n == 0:
                    kk = jnp.concatenate([kp_ref[r, :, cols], k_ref[r, rows, cols]], axis=0)
                    vv = jnp.concatenate([vp_ref[r, :, cols], v_ref[r, rows, cols]], axis=0)
                else:
                    both = slice((n - 1) * BLOCK, (n + 1) * BLOCK)
                    kk = k_ref[r, both, cols]
                    vv = v_ref[r, both, cols]
                s = jax.lax.dot_general(q, kk, nt, preferred_element_type=F32) * scale + biases[h]
                if n == 0:
                    s = jnp.where(jnp.logical_or(own_block, chunk_has_prev), s, -jnp.inf)
                m = jnp.max(s, axis=-1, keepdims=True)
                p = jnp.exp(s - m)
                denom = jnp.sum(p, axis=-1, keepdims=True)
                acc = jnp.dot(p.astype(BF16), vv, preferred_element_type=F32)
                start = (n * BLOCK) * dilation + r
                o_ref[h, pl.ds(start, BLOCK, stride=dilation), :] = acc / denom
                lse = m + jnp.log(denom)
                mine = jnp.logical_and(lane >= h * LSE_LANES, lane < (h + 1) * LSE_LANES)
                lse_tile = jnp.where(mine, lse, lse_tile)
            l_ref[pl.ds((n * BLOCK) * dilation + r, BLOCK, stride=dilation), :] = lse_tile


def _attn_group(qkv, gi, batch, seq):
    _, dilation = ATTN_GROUPS[gi]
    per = ATTN_CHUNK // dilation
    blocks_per_chunk = per // BLOCK
    slopes = _alibi_slopes()[gi * HEADS_PER_GROUP:(gi + 1) * HEADS_PER_GROUP]

    def cur(part):
        return pl.BlockSpec((None, dilation, per, GROUP_WIDTH), lambda b, c: (b, 0, c, part))

    def prev(part):
        return pl.BlockSpec((None, dilation, BLOCK, GROUP_WIDTH),
                            lambda b, c: (b, 0, jnp.maximum(c * blocks_per_chunk - 1, 0), part))

    return pl.pallas_call(
        functools.partial(_attn_body, slopes=slopes, dilation=dilation),
        grid=(batch, seq // ATTN_CHUNK),
        in_specs=[cur(0), cur(1), cur(2), prev(1), prev(2)],
        out_specs=[pl.BlockSpec((None, HEADS_PER_GROUP, ATTN_CHUNK, HEAD_DIM), lambda b, c: (b, 0, c, 0)),
                   pl.BlockSpec((None, ATTN_CHUNK, LANES), lambda b, c: (b, c, 0))],
        out_shape=[jax.ShapeDtypeStruct((batch, HEADS_PER_GROUP, seq, HEAD_DIM), F32),
                   jax.ShapeDtypeStruct((batch, seq, LANES), F32)],
        compiler_params=_params(2),
        name=f"attn_d{dilation}",
    )(qkv, qkv, qkv, qkv, qkv)


def _tail_body(x_ref, c_ref, sgc_ref, sga_ref, o0_ref, o1_ref, o2_ref, l0_ref, l1_ref, l2_ref,
               lg_ref, lb_ref, wc_ref, wa_ref, wo_ref, ng_ref, wg_ref, wu_ref, wd_ref, fg_ref, out_ref,
               y_ref, attn_ref, act_ref):
    c = c_ref[...]
    mu = jnp.mean(c, axis=-1, keepdims=True)
    d = c - mu
    var = jnp.mean(d * d, axis=-1, keepdims=True)
    y = d * jax.lax.rsqrt(var + EPS) * lg_ref[...] + lb_ref[...]
    y_ref[...] = (y * jax.nn.sigmoid(y)).astype(BF16)
    mixed = sgc_ref[...] * jnp.dot(y_ref[...], wc_ref[...], preferred_element_type=F32)

    l0, l1, l2 = l0_ref[...], l1_ref[...], l2_ref[...]
    mx = jnp.maximum(jnp.maximum(l0, l1), l2)
    e0, e1, e2 = jnp.exp(l0 - mx), jnp.exp(l1 - mx), jnp.exp(l2 - mx)
    tot = e0 + e1 + e2
    for h in range(HEADS_PER_GROUP):
        pick = slice(h * LSE_LANES, h * LSE_LANES + 1)
        acc = (e0[:, pick] / tot[:, pick]) * o0_ref[h]
        acc = acc + (e1[:, pick] / tot[:, pick]) * o1_ref[h]
        acc = acc + (e2[:, pick] / tot[:, pick]) * o2_ref[h]
        attn_ref[:, h * HEAD_DIM:(h + 1) * HEAD_DIM] = acc.astype(BF16)
    mixed = mixed + sga_ref[...] * jnp.dot(attn_ref[...], wa_ref[...], preferred_element_type=F32)
    x = x_ref[...] + jnp.dot(mixed.astype(BF16), wo_ref[...], preferred_element_type=F32)

    h_in = _rmsnorm(x, ng_ref[...]).astype(BF16)
    for ch in range(D_FF // FF_CHUNK):
        _swiglu_chunk(h_in, wg_ref, wu_ref, act_ref, ch)
    out = x + 0.5 * jnp.dot(act_ref[...], wd_ref[...], preferred_element_type=F32)
    out_ref[...] = _rmsnorm(out, fg_ref[...])


def _tail(x, c, sgc, sga, outs, lses, lg, lb, wc, wa, wo, ng, wg, wu, wd, fg, batch, seq):
    tokens = batch * seq
    tps = seq // TAIL_TILE
    tile = pl.BlockSpec((TAIL_TILE, D_MODEL), lambda i: (i, 0))
    o_tile = pl.BlockSpec((None, HEADS_PER_GROUP, TAIL_TILE, HEAD_DIM), lambda i: (i // tps, 0, i % tps, 0))
    l_tile = pl.BlockSpec((None, TAIL_TILE, LANES), lambda i: (i // tps, i % tps, 0))
    row = _const_spec((1, D_MODEL))
    return pl.pallas_call(
        _tail_body,
        grid=(tokens // TAIL_TILE,),
        in_specs=[tile, tile, tile, tile, o_tile, o_tile, o_tile, l_tile, l_tile, l_tile,
                  row, row, _const_spec((D_MODEL, D_MODEL)), _const_spec((GROUP_WIDTH, D_MODEL)),
                  _const_spec((D_MODEL, D_MODEL)),
                  row, _const_spec((D_MODEL, D_FF)), _const_spec((D_MODEL, D_FF)),
                  _const_spec((D_FF, D_MODEL)), row],
        out_specs=tile,
        out_shape=jax.ShapeDtypeStruct((tokens, D_MODEL), F32),
        scratch_shapes=[pltpu.VMEM((TAIL_TILE, D_MODEL), BF16),
                        pltpu.VMEM((TAIL_TILE, GROUP_WIDTH), BF16),
                        pltpu.VMEM((TAIL_TILE, D_FF), BF16)],
        compiler_params=_params(1),
        name="tail",
    )(x, c, sgc, sga, *outs, *lses, lg, lb, wc, wa, wo, ng, wg, wu, wd, fg)


def kernel(x, ffn1_norm, ffn1_w_gate, ffn1_w_up, ffn1_w_down, mix_norm, w_in, conv_dw_kernel, conv_dw_bias, conv_ln_gain, conv_ln_bias, conv_w_out, attn_w_out, w_o, ffn2_norm, ffn2_w_gate, ffn2_w_up, ffn2_w_down, final_norm):
    batch, seq, _ = x.shape
    depth = ffn1_norm.shape[0]
    assert depth == 1, "the tail kernel applies the final norm, so it closes the single layer"
    assert seq % ATTN_CHUNK == 0 and seq % TOKEN_TILE == 0 and seq % TAIL_TILE == 0
    tokens = batch * seq
    l = 0
    xt = x.reshape(tokens, D_MODEL)
    xt = _ffn(xt, ffn1_norm[l][None], ffn1_w_gate[l].astype(BF16), ffn1_w_up[l].astype(BF16),
              ffn1_w_down[l].astype(BF16))
    conv, qkv0, qkv1, qkv2, sgc, sga = _proj(xt, mix_norm[l][None], w_in[l].astype(BF16),
                                             conv_dw_kernel[l], conv_dw_bias[l][None], batch, seq)
    outs, lses = [], []
    for gi, qkv in enumerate((qkv0, qkv1, qkv2)):
        o_g, l_g = _attn_group(qkv, gi, batch, seq)
        outs.append(o_g)
        lses.append(l_g)
    out = _tail(xt, conv.reshape(tokens, D_MODEL), sgc.reshape(tokens, D_MODEL),
                sga.reshape(tokens, D_MODEL), outs, lses, conv_ln_gain[l][None], conv_ln_bias[l][None],
                conv_w_out[l].astype(BF16), attn_w_out[l].astype(BF16), w_o[l].astype(BF16),
                ffn2_norm[l][None], ffn2_w_gate[l].astype(BF16), ffn2_w_up[l].astype(BF16),
                ffn2_w_down[l].astype(BF16), final_norm[None], batch, seq)
    return out.reshape(batch, seq, D_MODEL)
```

```python
import functools

import jax
import jax.numpy as jnp
import numpy as np
from jax.experimental import pallas as pl
from jax.experimental.pallas import tpu as pltpu

F32 = jnp.float32
BF16 = jnp.bfloat16

D_MODEL = 1024
D_FF = 2816
CONV_WIDTH = 31
HEAD_DIM = 128
HEADS_PER_GROUP = 4
GROUP_WIDTH = HEADS_PER_GROUP * HEAD_DIM
ATTN_GROUPS = ((128, 1), (512, 4), (2048, 16))
N_GROUPS = len(ATTN_GROUPS)
N_ATTN_HEADS = HEADS_PER_GROUP * N_GROUPS
ATTN_WIDTH = N_ATTN_HEADS * HEAD_DIM
QKV_WIDTH = 3 * GROUP_WIDTH
BLOCK = 128
EPS = 1e-6

LANES = 128
SUBLANES = 8
LANE_BLOCKS = D_MODEL // LANES
VMEM_LIMIT_BYTES = 56 * 1024 * 1024

TOKEN_TILE = 512
TAIL_TILE = 256
FF_CHUNK = 256
PROJ_CHUNK = 512
CONV_HALO = 32
CONV_ROWS = 64
TAIL_CONV_SHARES = (3,) * 10 + (2,)
ATTN_CHUNK = ATTN_GROUPS[-1][1] * BLOCK
LSE_LANES = LANES // HEADS_PER_GROUP


def _alibi_slopes():
    h = np.arange(1, N_ATTN_HEADS + 1, dtype=np.float32)
    return np.power(np.float32(2.0), -8.0 * h / np.float32(N_ATTN_HEADS)).astype(np.float32)


def _rmsnorm(x, g):
    ms = jnp.mean(x * x, axis=-1, keepdims=True)
    return x * jax.lax.rsqrt(ms + EPS) * g


def _const_spec(shape):
    return pl.BlockSpec(shape, lambda *_: (0,) * len(shape), pipeline_mode=pl.Buffered(1))


def _params(n_axes):
    return pltpu.CompilerParams(
        dimension_semantics=("arbitrary",) * n_axes,
        vmem_limit_bytes=VMEM_LIMIT_BYTES)


def _swiglu_chunk(h, wg_ref, wu_ref, act_ref, c):
    sl = slice(c * FF_CHUNK, (c + 1) * FF_CHUNK)
    g = jnp.dot(h, wg_ref[:, sl], preferred_element_type=F32)
    u = jnp.dot(h, wu_ref[:, sl], preferred_element_type=F32)
    a = g * jax.nn.sigmoid(g) * u
    act_ref[:, sl] = a.astype(BF16)
    return a[0:SUBLANES, 0:LANES]


def _ffn_body(x_ref, g_ref, wg_ref, wu_ref, wd_ref, o_ref, act_ref):
    x = x_ref[...]
    h = _rmsnorm(x, g_ref[...]).astype(BF16)
    for c in range(D_FF // FF_CHUNK):
        _swiglu_chunk(h, wg_ref, wu_ref, act_ref, c)
    y = jnp.dot(act_ref[...], wd_ref[...], preferred_element_type=F32)
    o_ref[...] = x + 0.5 * y


def _ffn(x, norm_g, wg, wu, wd):
    t = x.shape[0]
    tile = pl.BlockSpec((TOKEN_TILE, D_MODEL), lambda i: (i, 0))
    return pl.pallas_call(
        _ffn_body,
        grid=(t // TOKEN_TILE,),
        in_specs=[tile, _const_spec((1, D_MODEL)), _const_spec((D_MODEL, D_FF)),
                  _const_spec((D_MODEL, D_FF)), _const_spec((D_FF, D_MODEL))],
        out_specs=tile,
        out_shape=jax.ShapeDtypeStruct((t, D_MODEL), F32),
        scratch_shapes=[pltpu.VMEM((TOKEN_TILE, D_FF), BF16)],
        compiler_params=_params(1),
        name="ffn",
    )(x, norm_g, wg, wu, wd)


def _proj_body(x_ref, g_ref, w_ref, z_ref, qkv0_ref, qkv1_ref, qkv2_ref, sgc_ref, sga_ref,
               xs_ref, hp_ref):
    xn = _rmsnorm(x_ref[...], g_ref[...])
    h = xn.astype(BF16)

    def mm(lhs, lo, n=PROJ_CHUNK):
        return jnp.dot(lhs, w_ref[:, lo:lo + n], preferred_element_type=F32)

    for c in range(D_MODEL // PROJ_CHUNK):
        sl = slice(c * PROJ_CHUNK, (c + 1) * PROJ_CHUNK)
        a = mm(h, c * PROJ_CHUNK)
        b = mm(h, D_MODEL + c * PROJ_CHUNK)
        z_ref[:, sl] = a * jax.nn.sigmoid(b)
    gate_base = 2 * D_MODEL + 3 * ATTN_WIDTH
    for j, ref in enumerate((sgc_ref, sga_ref)):
        for c in range(D_MODEL // PROJ_CHUNK):
            sl = slice(c * PROJ_CHUNK, (c + 1) * PROJ_CHUNK)
            ref[:, sl] = jax.nn.sigmoid(mm(h, gate_base + j * D_MODEL + c * PROJ_CHUNK))

    for c in range(LANE_BLOCKS):
        xs_ref[c] = xn[:, c * LANES:(c + 1) * LANES]

    for gi, ref in enumerate((qkv0_ref, qkv1_ref, qkv2_ref)):
        dilation = ATTN_GROUPS[gi][1]
        per = TOKEN_TILE // dilation
        if dilation == 1:
            lhs = h
        else:
            for r in range(dilation):
                for c in range(LANE_BLOCKS):
                    rows = xs_ref[c, pl.ds(r, per, stride=dilation), :]
                    hp_ref[r * per:(r + 1) * per, c * LANES:(c + 1) * LANES] = rows.astype(BF16)
            lhs = hp_ref[...]
        for part in range(3):
            res = mm(lhs, 2 * D_MODEL + part * ATTN_WIDTH + gi * GROUP_WIDTH, GROUP_WIDTH).astype(BF16)
            for r in range(dilation):
                ref[r, :, part * GROUP_WIDTH:(part + 1) * GROUP_WIDTH] = res[r * per:(r + 1) * per]


def _proj(x, norm_g, w_in, batch, seq):
    in_width = w_in.shape[1]
    tiles = seq // TOKEN_TILE
    x3 = x.reshape(batch, seq, D_MODEL)
    tile = pl.BlockSpec((None, TOKEN_TILE, D_MODEL), lambda b, j: (b, j, 0))
    wide = jax.ShapeDtypeStruct((batch, seq, D_MODEL), F32)
    qkv_specs, qkv_shapes = [], []
    for _, dilation in ATTN_GROUPS:
        qkv_specs.append(pl.BlockSpec((None, dilation, TOKEN_TILE // dilation, QKV_WIDTH),
                                      lambda b, j: (b, 0, j, 0)))
        qkv_shapes.append(jax.ShapeDtypeStruct((batch, dilation, seq // dilation, QKV_WIDTH), BF16))
    return pl.pallas_call(
        _proj_body,
        grid=(batch, tiles),
        in_specs=[tile, _const_spec((1, D_MODEL)), _const_spec((D_MODEL, in_width))],
        out_specs=[tile, *qkv_specs, tile, tile],
        out_shape=[wide, *qkv_shapes, wide, wide],
        scratch_shapes=[pltpu.VMEM((LANE_BLOCKS, TOKEN_TILE, LANES), F32),
                        pltpu.VMEM((TOKEN_TILE, D_MODEL), BF16)],
        compiler_params=_params(2),
        name="proj",
    )(x3, norm_g, w_in)


def _attn_body(q_ref, k_ref, v_ref, kp_ref, vp_ref, o_ref, l_ref, *, slopes, dilation):
    per = ATTN_CHUNK // dilation
    chunk_has_prev = pl.program_id(1) > 0
    qi = jax.lax.broadcasted_iota(jnp.int32, (BLOCK, 2 * BLOCK), 0)
    kj = jax.lax.broadcasted_iota(jnp.int32, (BLOCK, 2 * BLOCK), 1)
    steps = BLOCK + qi - kj
    in_band = jnp.logical_and(steps >= 0, steps <= BLOCK)
    own_block = kj >= BLOCK
    lane = jax.lax.broadcasted_iota(jnp.int32, (BLOCK, LANES), 1)
    scale = HEAD_DIM ** -0.5
    nt = (((1,), (1,)), ((), ()))
    biases = []
    for h in range(HEADS_PER_GROUP):
        alibi = (-float(slopes[h]) * dilation) * steps.astype(F32)
        biases.append(jnp.where(in_band, alibi, -jnp.inf))
    for r in range(dilation):
        for n in range(per // BLOCK):
            rows = slice(n * BLOCK, (n + 1) * BLOCK)
            lse_tile = jnp.zeros((BLOCK, LANES), F32)
            for h in range(HEADS_PER_GROUP):
                cols = slice(h * HEAD_DIM, (h + 1) * HEAD_DIM)
                q = q_ref[r, rows, cols]
                if n == 0:
                    kk = jnp.concatenate([kp_ref[r, :, cols], k_ref[r, rows, cols]], axis=0)
                    vv = jnp.concatenate([vp_ref[r, :, cols], v_ref[r, rows, cols]], axis=0)
                else:
                    both = slice((n - 1) * BLOCK, (n + 1) * BLOCK)
                    kk = k_ref[r, both, cols]
                    vv = v_ref[r, both, cols]
                s = jax.lax.dot_general(q, kk, nt, preferred_element_type=F32) * scale + biases[h]
                if n == 0:
                    s = jnp.where(jnp.logical_or(own_block, chunk_has_prev), s, -jnp.inf)
                m = jnp.max(s, axis=-1, keepdims=True)
                p = jnp.exp(s - m)
                denom = jnp.sum(p, axis=-1, keepdims=True)
                acc = jnp.dot(p.astype(BF16), vv, preferred_element_type=F32)
                start = (n * BLOCK) * dilation + r
                o_ref[h, pl.ds(start, BLOCK, stride=dilation), :] = acc / denom
                lse = m + jnp.log(denom)
                mine = jnp.logical_and(lane >= h * LSE_LANES, lane < (h + 1) * LSE_LANES)
                lse_tile = jnp.where(mine, lse, lse_tile)
            l_ref[pl.ds((n * BLOCK) * dilation + r, BLOCK, stride=dilation), :] = lse_tile


def _attn_group(qkv, gi, batch, seq):
    _, dilation = ATTN_GROUPS[gi]
    per = ATTN_CHUNK // dilation
    blocks_per_chunk = per // BLOCK
    slopes = _alibi_slopes()[gi * HEADS_PER_GROUP:(gi + 1) * HEADS_PER_GROUP]

    def cur(part):
        return pl.BlockSpec((None, dilation, per, GROUP_WIDTH), lambda b, c: (b, 0, c, part))

    def prev(part):
        return pl.BlockSpec((None, dilation, BLOCK, GROUP_WIDTH),
                            lambda b, c: (b, 0, jnp.maximum(c * blocks_per_chunk - 1, 0), part))

    return pl.pallas_call(
        functools.partial(_attn_body, slopes=slopes, dilation=dilation),
        grid=(batch, seq // ATTN_CHUNK),
        in_specs=[cur(0), cur(1), cur(2), prev(1), prev(2)],
        out_specs=[pl.BlockSpec((None, HEADS_PER_GROUP, ATTN_CHUNK, HEAD_DIM), lambda b, c: (b, 0, c, 0)),
                   pl.BlockSpec((None, ATTN_CHUNK, LANES), lambda b, c: (b, c, 0))],
        out_shape=[jax.ShapeDtypeStruct((batch, HEADS_PER_GROUP, seq, HEAD_DIM), F32),
                   jax.ShapeDtypeStruct((batch, seq, LANES), F32)],
        compiler_params=_params(2),
        name=f"attn_d{dilation}",
    )(qkv, qkv, qkv, qkv, qkv)


def _anchored_zero(v):
    u = pltpu.bitcast(v, jnp.uint32)
    u = jax.lax.shift_right_logical(jax.lax.shift_right_logical(u, jnp.uint32(16)), jnp.uint32(16))
    return pltpu.bitcast(u, F32)


def _conv_piece(z_ref, halo, rc, cb, wb_ref, db_ref, anchor=None):
    cols = slice(cb * LANES, (cb + 1) * LANES)
    span = CONV_HALO + CONV_ROWS
    if rc == 0:
        window = jnp.concatenate([halo, z_ref[0:CONV_ROWS, cols]], axis=0)
    else:
        window = z_ref[rc * CONV_ROWS - CONV_HALO:(rc + 1) * CONV_ROWS, cols]
    if anchor is not None:
        window = window + jnp.concatenate([anchor] * (span // SUBLANES), axis=0)
    lead = CONV_HALO - (CONV_WIDTH - 1)
    acc = jnp.broadcast_to(db_ref[:, cols], (CONV_ROWS, LANES))
    for r in range(SUBLANES):
        shifted = window if r == 0 else pltpu.roll(window, span - r, axis=0)
        for a in range((span - CONV_ROWS) // SUBLANES + 1):
            k = a * SUBLANES + r - lead
            if 0 <= k < CONV_WIDTH:
                tap = jnp.concatenate([wb_ref[k, :, cols]] * (CONV_ROWS // SUBLANES), axis=0)
                acc = acc + shifted[a * SUBLANES:a * SUBLANES + CONV_ROWS] * tap
    return acc


def _tail_body(x_ref, sgc_ref, sga_ref, o0_ref, o1_ref, o2_ref, l0_ref, l1_ref, l2_ref,
               z0_ref, zn_ref, hn_ref, dw_ref, db_ref, lg_ref, lb_ref, wc_ref, wa_ref, wo_ref,
               ng_ref, wg_ref, wu_ref, wd_ref, fg_ref, out_ref,
               wb_ref, c_ref, y_ref, attn_ref, act_ref, hin_ref, *, tiles_per_seq):
    i = pl.program_id(0)
    pieces = [(rc, cb) for rc in range(TAIL_TILE // CONV_ROWS) for cb in range(LANE_BLOCKS)]

    @pl.when(i == 0)
    def _():
        for k in range(CONV_WIDTH):
            wb_ref[k] = jnp.broadcast_to(dw_ref[k:k + 1, :], (SUBLANES, D_MODEL))
        zero_halo = jnp.zeros((CONV_HALO, LANES), F32)
        for rc, cb in pieces:
            c_ref[rc * CONV_ROWS:(rc + 1) * CONV_ROWS, cb * LANES:(cb + 1) * LANES] = _conv_piece(
                z0_ref, zero_halo, rc, cb, wb_ref, db_ref)

    c = c_ref[...]
    mu = jnp.mean(c, axis=-1, keepdims=True)
    d = c - mu
    var = jnp.mean(d * d, axis=-1, keepdims=True)
    y = d * jax.lax.rsqrt(var + EPS) * lg_ref[...] + lb_ref[...]
    y_ref[...] = (y * jax.nn.sigmoid(y)).astype(BF16)
    mixed = sgc_ref[...] * jnp.dot(y_ref[...], wc_ref[...], preferred_element_type=F32)

    l0, l1, l2 = l0_ref[...], l1_ref[...], l2_ref[...]
    mx = jnp.maximum(jnp.maximum(l0, l1), l2)
    e0, e1, e2 = jnp.exp(l0 - mx), jnp.exp(l1 - mx), jnp.exp(l2 - mx)
    tot = e0 + e1 + e2
    for h in range(HEADS_PER_GROUP):
        pick = slice(h * LSE_LANES, h * LSE_LANES + 1)
        acc = (e0[:, pick] / tot[:, pick]) * o0_ref[h]
        acc = acc + (e1[:, pick] / tot[:, pick]) * o1_ref[h]
        acc = acc + (e2[:, pick] / tot[:, pick]) * o2_ref[h]
        attn_ref[:, h * HEAD_DIM:(h + 1) * HEAD_DIM] = acc.astype(BF16)
    mixed = mixed + sga_ref[...] * jnp.dot(attn_ref[...], wa_ref[...], preferred_element_type=F32)
    x = x_ref[...] + jnp.dot(mixed.astype(BF16), wo_ref[...], preferred_element_type=F32)

    nxt_first = (i + 1) % tiles_per_seq == 0
    hin_ref[...] = _rmsnorm(x, ng_ref[...]).astype(BF16)
    chain = None
    done = [None]
    nxt = 0
    for ch in range(D_FF // FF_CHUNK):
        if len(done) >= 2 and done[-2] is not None:
            rows = hin_ref[0:2 * SUBLANES, 0:LANES].astype(F32)
            zero = _anchored_zero(done[-2])
            hin_ref[0:2 * SUBLANES, 0:LANES] = (rows + jnp.concatenate([zero, zero], axis=0)).astype(BF16)
        anchor = _anchored_zero(_swiglu_chunk(hin_ref[...], wg_ref, wu_ref, act_ref, ch))
        for rc, cb in pieces[nxt:nxt + TAIL_CONV_SHARES[ch]]:
            if chain is not None:
                anchor = anchor + _anchored_zero(chain)
            halo = jnp.where(nxt_first, 0.0, hn_ref[:, cb * LANES:(cb + 1) * LANES]) if rc == 0 else None
            acc = _conv_piece(zn_ref, halo, rc, cb, wb_ref, db_ref, anchor)
            c_ref[rc * CONV_ROWS:(rc + 1) * CONV_ROWS, cb * LANES:(cb + 1) * LANES] = acc
            chain = acc[0:SUBLANES]
            anchor = jnp.zeros((SUBLANES, LANES), F32)
        nxt += TAIL_CONV_SHARES[ch]
        done.append(chain)
    assert nxt == len(pieces)
    out = x + 0.5 * jnp.dot(act_ref[...], wd_ref[...], preferred_element_type=F32)
    out_ref[...] = _rmsnorm(out, fg_ref[...])


def _tail(x, sgc, sga, outs, lses, z, dw, db, lg, lb, wc, wa, wo, ng, wg, wu, wd, fg, batch, seq):
    tokens = batch * seq
    n_tiles = tokens // TAIL_TILE
    tps = seq // TAIL_TILE
    halo_per_tile = TAIL_TILE // CONV_HALO
    tile = pl.BlockSpec((TAIL_TILE, D_MODEL), lambda i: (i, 0))
    o_tile = pl.BlockSpec((None, HEADS_PER_GROUP, TAIL_TILE, HEAD_DIM), lambda i: (i // tps, 0, i % tps, 0))
    l_tile = pl.BlockSpec((None, TAIL_TILE, LANES), lambda i: (i // tps, i % tps, 0))
    z_first = pl.BlockSpec((TAIL_TILE, D_MODEL), lambda i: (0, 0), pipeline_mode=pl.Buffered(1))
    z_next = pl.BlockSpec((TAIL_TILE, D_MODEL), lambda i: (jnp.minimum(i + 1, n_tiles - 1), 0))
    h_next = pl.BlockSpec((CONV_HALO, D_MODEL),
                          lambda i: (jnp.minimum(i + 1, n_tiles - 1) * halo_per_tile - 1, 0))
    row = _const_spec((1, D_MODEL))
    return pl.pallas_call(
        functools.partial(_tail_body, tiles_per_seq=tps),
        grid=(n_tiles,),
        in_specs=[tile, tile, tile, o_tile, o_tile, o_tile, l_tile, l_tile, l_tile,
                  z_first, z_next, h_next,
                  _const_spec((CONV_WIDTH, D_MODEL)), row, row, row,
                  _const_spec((D_MODEL, D_MODEL)), _const_spec((GROUP_WIDTH, D_MODEL)),
                  _const_spec((D_MODEL, D_MODEL)),
                  row, _const_spec((D_MODEL, D_FF)), _const_spec((D_MODEL, D_FF)),
                  _const_spec((D_FF, D_MODEL)), row],
        out_specs=tile,
        out_shape=jax.ShapeDtypeStruct((tokens, D_MODEL), F32),
        scratch_shapes=[pltpu.VMEM((CONV_WIDTH, SUBLANES, D_MODEL), F32),
                        pltpu.VMEM((TAIL_TILE, D_MODEL), F32),
                        pltpu.VMEM((TAIL_TILE, D_MODEL), BF16),
                        pltpu.VMEM((TAIL_TILE, GROUP_WIDTH), BF16),
                        pltpu.VMEM((TAIL_TILE, D_FF), BF16),
                        pltpu.VMEM((TAIL_TILE, D_MODEL), BF16)],
        compiler_params=_params(1),
        name="tail",
    )(x, sgc, sga, *outs, *lses, z, z, z, dw, db, lg, lb, wc, wa, wo, ng, wg, wu, wd, fg)


def kernel(x, ffn1_norm, ffn1_w_gate, ffn1_w_up, ffn1_w_down, mix_norm, w_in, conv_dw_kernel, conv_dw_bias, conv_ln_gain, conv_ln_bias, conv_w_out, attn_w_out, w_o, ffn2_norm, ffn2_w_gate, ffn2_w_up, ffn2_w_down, final_norm):
    batch, seq, _ = x.shape
    depth = ffn1_norm.shape[0]
    assert depth == 1, "the tail kernel applies the final norm, so it closes the single layer"
    assert seq % ATTN_CHUNK == 0 and seq % TOKEN_TILE == 0 and seq % TAIL_TILE == 0
    tokens = batch * seq
    l = 0
    xt = x.reshape(tokens, D_MODEL)
    xt = _ffn(xt, ffn1_norm[l][None], ffn1_w_gate[l].astype(BF16), ffn1_w_up[l].astype(BF16),
              ffn1_w_down[l].astype(BF16))
    z, qkv0, qkv1, qkv2, sgc, sga = _proj(xt, mix_norm[l][None], w_in[l].astype(BF16), batch, seq)
    outs, lses = [], []
    for gi, qkv in enumerate((qkv0, qkv1, qkv2)):
        o_g, l_g = _attn_group(qkv, gi, batch, seq)
        outs.append(o_g)
        lses.append(l_g)
    out = _tail(xt, sgc.reshape(tokens, D_MODEL), sga.reshape(tokens, D_MODEL), outs, lses,
                z.reshape(tokens, D_MODEL), conv_dw_kernel[l], conv_dw_bias[l][None],
                conv_ln_gain[l][None], conv_ln_bias[l][None], conv_w_out[l].astype(BF16),
                attn_w_out[l].astype(BF16), w_o[l].astype(BF16), ffn2_norm[l][None],
                ffn2_w_gate[l].astype(BF16), ffn2_w_up[l].astype(BF16), ffn2_w_down[l].astype(BF16),
                final_norm[None], batch, seq)
    return out.reshape(batch, seq, D_MODEL)
```

```python
import functools

import jax
import jax.numpy as jnp
import numpy as np
from jax.experimental import pallas as pl
from jax.experimental.pallas import tpu as pltpu

F32 = jnp.float32
BF16 = jnp.bfloat16

D_MODEL = 1024
D_FF = 2816
CONV_WIDTH = 31
HEAD_DIM = 128
HEADS_PER_GROUP = 4
GROUP_WIDTH = HEADS_PER_GROUP * HEAD_DIM
ATTN_GROUPS = ((128, 1), (512, 4), (2048, 16))
N_GROUPS = len(ATTN_GROUPS)
N_ATTN_HEADS = HEADS_PER_GROUP * N_GROUPS
ATTN_WIDTH = N_ATTN_HEADS * HEAD_DIM
QKV_WIDTH = 3 * GROUP_WIDTH
BLOCK = 128
EPS = 1e-6

LANES = 128
SUBLANES = 8
LANE_BLOCKS = D_MODEL // LANES
VMEM_LIMIT_BYTES = 60 * 1024 * 1024

FFN_TILE = 1024
TOKEN_TILE = 512
TAIL_TILE = 512
FF_CHUNK = 256
PROJ_CHUNK = 512
CONV_HALO = 32
CONV_ROWS = 64
CONV_SHARES = (7,) * 4 + (6,) * 6 + (0,)
ATTN_CHUNK = ATTN_GROUPS[-1][1] * BLOCK
LSE_LANES = LANES // HEADS_PER_GROUP


def _alibi_slopes():
    h = np.arange(1, N_ATTN_HEADS + 1, dtype=np.float32)
    return np.power(np.float32(2.0), -8.0 * h / np.float32(N_ATTN_HEADS)).astype(np.float32)


def _rmsnorm(x, g):
    ms = jnp.mean(x * x, axis=-1, keepdims=True)
    return x * jax.lax.rsqrt(ms + EPS) * g


def _const_spec(shape):
    return pl.BlockSpec(shape, lambda *_: (0,) * len(shape), pipeline_mode=pl.Buffered(1))


def _params(n_axes):
    return pltpu.CompilerParams(
        dimension_semantics=("arbitrary",) * n_axes,
        vmem_limit_bytes=VMEM_LIMIT_BYTES)


def _swiglu_chunk(h, wg_ref, wu_ref, act_ref, c):
    sl = slice(c * FF_CHUNK, (c + 1) * FF_CHUNK)
    g = jnp.dot(h, wg_ref[:, sl], preferred_element_type=F32)
    u = jnp.dot(h, wu_ref[:, sl], preferred_element_type=F32)
    act_ref[:, sl] = (g * jax.nn.sigmoid(g) * u).astype(BF16)


def _ffn_body(x_ref, g_ref, wg_ref, wu_ref, wd_ref, o_ref, act_ref):
    x = x_ref[...]
    h = _rmsnorm(x, g_ref[...]).astype(BF16)
    for c in range(D_FF // FF_CHUNK):
        _swiglu_chunk(h, wg_ref, wu_ref, act_ref, c)
    y = jnp.dot(act_ref[...], wd_ref[...], preferred_element_type=F32)
    o_ref[...] = x + 0.5 * y


def _ffn(x, norm_g, wg, wu, wd):
    t = x.shape[0]
    tile = pl.BlockSpec((FFN_TILE, D_MODEL), lambda i: (i, 0))
    return pl.pallas_call(
        _ffn_body,
        grid=(t // FFN_TILE,),
        in_specs=[tile, _const_spec((1, D_MODEL)), _const_spec((D_MODEL, D_FF)),
                  _const_spec((D_MODEL, D_FF)), _const_spec((D_FF, D_MODEL))],
        out_specs=tile,
        out_shape=jax.ShapeDtypeStruct((t, D_MODEL), F32),
        scratch_shapes=[pltpu.VMEM((FFN_TILE, D_FF), BF16)],
        compiler_params=_params(1),
        name="ffn",
    )(x, norm_g, wg, wu, wd)


def _anchored_zero(v):
    u = pltpu.bitcast(v, jnp.uint32)
    u = jax.lax.shift_right_logical(jax.lax.shift_right_logical(u, jnp.uint32(16)), jnp.uint32(16))
    return pltpu.bitcast(u, F32)


def _conv_piece(zz_ref, rc, cb, wb_ref, db_ref, anchor):
    cols = slice(cb * LANES, (cb + 1) * LANES)
    span = CONV_HALO + CONV_ROWS
    window = zz_ref[rc * CONV_ROWS:rc * CONV_ROWS + span, cols]
    window = window + jnp.concatenate([anchor] * (span // SUBLANES), axis=0)
    lead = CONV_HALO - (CONV_WIDTH - 1)
    acc = jnp.broadcast_to(db_ref[:, cols], (CONV_ROWS, LANES))
    for r in range(SUBLANES):
        shifted = window if r == 0 else pltpu.roll(window, span - r, axis=0)
        for a in range((span - CONV_ROWS) // SUBLANES + 1):
            k = a * SUBLANES + r - lead
            if 0 <= k < CONV_WIDTH:
                tap = jnp.concatenate([wb_ref[k, :, cols]] * (CONV_ROWS // SUBLANES), axis=0)
                acc = acc + shifted[a * SUBLANES:a * SUBLANES + CONV_ROWS] * tap
    return acc


def _proj_body(x_ref, g_ref, w_ref, dw_ref, db_ref, c_ref, qkv0_ref, qkv1_ref, qkv2_ref,
               h_ref, xs_ref, hp_ref, zz_ref, wb_ref):
    first_step = jnp.logical_and(pl.program_id(0) == 0, pl.program_id(1) == 0)

    @pl.when(first_step)
    def _():
        for k in range(CONV_WIDTH):
            wb_ref[k] = jnp.broadcast_to(dw_ref[k:k + 1, :], (SUBLANES, D_MODEL))

    @pl.when(pl.program_id(1) == 0)
    def _():
        zz_ref[0:CONV_HALO, :] = jnp.zeros((CONV_HALO, D_MODEL), F32)

    xn = _rmsnorm(x_ref[...], g_ref[...])
    h_ref[...] = xn.astype(BF16)
    for c in range(LANE_BLOCKS):
        xs_ref[c] = xn[:, c * LANES:(c + 1) * LANES]

    lanes_per_chunk = PROJ_CHUNK // LANES
    pieces = [(rc, cb) for half in range(D_MODEL // PROJ_CHUNK)
              for rc in range(TOKEN_TILE // CONV_ROWS)
              for cb in range(half * lanes_per_chunk, (half + 1) * lanes_per_chunk)]
    state = {"next_piece": 0, "chain": None, "done": [None]}

    def dot(lhs_ref, lo, n=PROJ_CHUNK):
        done = state["done"]
        if len(done) >= 2 and done[-2] is not None:
            rows = lhs_ref[0:2 * SUBLANES, 0:LANES].astype(F32)
            zero = _anchored_zero(done[-2])
            lhs_ref[0:2 * SUBLANES, 0:LANES] = (rows + jnp.concatenate([zero, zero], axis=0)).astype(BF16)
        return jnp.dot(lhs_ref[...], w_ref[:, lo:lo + n], preferred_element_type=F32)

    def conv_share(result, count):
        anchor = _anchored_zero(result[0:SUBLANES, 0:LANES])
        for rc, cb in pieces[state["next_piece"]:state["next_piece"] + count]:
            if state["chain"] is not None:
                anchor = anchor + _anchored_zero(state["chain"])
            acc = _conv_piece(zz_ref, rc, cb, wb_ref, db_ref, anchor)
            c_ref[rc * CONV_ROWS:(rc + 1) * CONV_ROWS, cb * LANES:(cb + 1) * LANES] = acc
            state["chain"] = acc[0:SUBLANES]
            anchor = jnp.zeros((SUBLANES, LANES), F32)
        state["next_piece"] += count
        state["done"].append(state["chain"])

    n_half = D_MODEL // PROJ_CHUNK
    shares = iter(CONV_SHARES)
    for c in range(n_half):
        sl = slice(c * PROJ_CHUNK, (c + 1) * PROJ_CHUNK)
        a = dot(h_ref, c * PROJ_CHUNK)
        if c > 0:
            conv_share(a, next(shares))
        b = dot(h_ref, D_MODEL + c * PROJ_CHUNK)
        z = a * jax.nn.sigmoid(b)
        zz_ref[CONV_HALO:, sl] = z
        if c > 0:
            conv_share(z, next(shares))

    for gi, ref in enumerate((qkv0_ref, qkv1_ref, qkv2_ref)):
        dilation = ATTN_GROUPS[gi][1]
        per = TOKEN_TILE // dilation
        if dilation == 1:
            lhs_ref = h_ref
        else:
            for r in range(dilation):
                for c in range(LANE_BLOCKS):
                    rows = xs_ref[c, pl.ds(r, per, stride=dilation), :]
                    hp_ref[r * per:(r + 1) * per, c * LANES:(c + 1) * LANES] = rows.astype(BF16)
            lhs_ref = hp_ref
        for part in range(3):
            res32 = dot(lhs_ref, 2 * D_MODEL + part * ATTN_WIDTH + gi * GROUP_WIDTH, GROUP_WIDTH)
            res = res32.astype(BF16)
            for r in range(dilation):
                ref[r, :, part * GROUP_WIDTH:(part + 1) * GROUP_WIDTH] = res[r * per:(r + 1) * per]
            conv_share(res32, next(shares))
    assert state["next_piece"] == len(pieces)

    zz_ref[0:CONV_HALO, :] = zz_ref[TOKEN_TILE:TOKEN_TILE + CONV_HALO, :]


def _proj(x, norm_g, w_in, dw, db, batch, seq):
    in_width = w_in.shape[1]
    tiles = seq // TOKEN_TILE
    x3 = x.reshape(batch, seq, D_MODEL)
    tile = pl.BlockSpec((None, TOKEN_TILE, D_MODEL), lambda b, j: (b, j, 0))
    wide = jax.ShapeDtypeStruct((batch, seq, D_MODEL), F32)
    qkv_specs, qkv_shapes = [], []
    for _, dilation in ATTN_GROUPS:
        qkv_specs.append(pl.BlockSpec((None, dilation, TOKEN_TILE // dilation, QKV_WIDTH),
                                      lambda b, j: (b, 0, j, 0)))
        qkv_shapes.append(jax.ShapeDtypeStruct((batch, dilation, seq // dilation, QKV_WIDTH), BF16))
    return pl.pallas_call(
        _proj_body,
        grid=(batch, tiles),
        in_specs=[tile, _const_spec((1, D_MODEL)), _const_spec((D_MODEL, in_width)),
                  _const_spec((CONV_WIDTH, D_MODEL)), _const_spec((1, D_MODEL))],
        out_specs=[tile, *qkv_specs],
        out_shape=[wide, *qkv_shapes],
        scratch_shapes=[pltpu.VMEM((TOKEN_TILE, D_MODEL), BF16),
                        pltpu.VMEM((LANE_BLOCKS, TOKEN_TILE, LANES), F32),
                        pltpu.VMEM((TOKEN_TILE, D_MODEL), BF16),
                        pltpu.VMEM((CONV_HALO + TOKEN_TILE, D_MODEL), F32),
                        pltpu.VMEM((CONV_WIDTH, SUBLANES, D_MODEL), F32)],
        compiler_params=_params(2),
        name="proj",
    )(x3, norm_g, w_in, dw, db)


def _attn_body(q_ref, k_ref, v_ref, kp_ref, vp_ref, o_ref, l_ref, *, slopes, dilation):
    per = ATTN_CHUNK // dilation
    chunk_has_prev = pl.program_id(1) > 0
    qi = jax.lax.broadcasted_iota(jnp.int32, (BLOCK, 2 * BLOCK), 0)
    kj = jax.lax.broadcasted_iota(jnp.int32, (BLOCK, 2 * BLOCK), 1)
    steps = BLOCK + qi - kj
    in_band = jnp.logical_and(steps >= 0, steps <= BLOCK)
    own_block = kj >= BLOCK
    lane = jax.lax.broadcasted_iota(jnp.int32, (BLOCK, LANES), 1)
    scale = HEAD_DIM ** -0.5
    nt = (((1,), (1,)), ((), ()))
    biases = []
    for h in range(HEADS_PER_GROUP):
        alibi = (-float(slopes[h]) * dilation) * steps.astype(F32)
        biases.append(jnp.where(in_band, alibi, -jnp.inf))
    for r in range(dilation):
        for n in range(per // BLOCK):
            rows = slice(n * BLOCK, (n + 1) * BLOCK)
            lse_tile = jnp.zeros((BLOCK, LANES), F32)
            for h in range(HEADS_PER_GROUP):
                cols = slice(h * HEAD_DIM, (h + 1) * HEAD_DIM)
                q = q_ref[r, rows, cols]
                if n == 0:
                    kk = jnp.concatenate([kp_ref[r, :, cols], k_ref[r, rows, cols]], axis=0)
                    vv = jnp.concatenate([vp_ref[r, :, cols], v_ref[r, rows, cols]], axis=0)
                else:
                    both = slice((n - 1) * BLOCK, (n + 1) * BLOCK)
                    kk = k_ref[r, both, cols]
                    vv = v_ref[r, both, cols]
                s = jax.lax.dot_general(q, kk, nt, preferred_element_type=F32) * scale + biases[h]
                if n == 0:
                    s = jnp.where(jnp.logical_or(own_block, chunk_has_prev), s, -jnp.inf)
                m = jnp.max(s, axis=-1, keepdims=True)
                p = jnp.exp(s - m)
                denom = jnp.sum(p, axis=-1, keepdims=True)
                acc = jnp.dot(p.astype(BF16), vv, preferred_element_type=F32)
                start = (n * BLOCK) * dilation + r
                o_ref[h, pl.ds(start, BLOCK, stride=dilation), :] = acc / denom
                lse = m + jnp.log(denom)
                mine = jnp.logical_and(lane >= h * LSE_LANES, lane < (h + 1) * LSE_LANES)
                lse_tile = jnp.where(mine, lse, lse_tile)
            l_ref[pl.ds((n * BLOCK) * dilation + r, BLOCK, stride=dilation), :] = lse_tile


def _attn_group(qkv, gi, batch, seq):
    _, dilation = ATTN_GROUPS[gi]
    per = ATTN_CHUNK // dilation
    blocks_per_chunk = per // BLOCK
    slopes = _alibi_slopes()[gi * HEADS_PER_GROUP:(gi + 1) * HEADS_PER_GROUP]

    def cur(part):
        return pl.BlockSpec((None, dilation, per, GROUP_WIDTH), lambda b, c: (b, 0, c, part))

    def prev(part):
        return pl.BlockSpec((None, dilation, BLOCK, GROUP_WIDTH),
                            lambda b, c: (b, 0, jnp.maximum(c * blocks_per_chunk - 1, 0), part))

    return pl.pallas_call(
        functools.partial(_attn_body, slopes=slopes, dilation=dilation),
        grid=(batch, seq // ATTN_CHUNK),
        in_specs=[cur(0), cur(1), cur(2), prev(1), prev(2)],
        out_specs=[pl.BlockSpec((None, HEADS_PER_GROUP, ATTN_CHUNK, HEAD_DIM), lambda b, c: (b, 0, c, 0)),
                   pl.BlockSpec((None, ATTN_CHUNK, LANES), lambda b, c: (b, c, 0))],
        out_shape=[jax.ShapeDtypeStruct((batch, HEADS_PER_GROUP, seq, HEAD_DIM), F32),
                   jax.ShapeDtypeStruct((batch, seq, LANES), F32)],
        compiler_params=_params(2),
        name=f"attn_d{dilation}",
    )(qkv, qkv, qkv, qkv, qkv)


def _tail_body(x_ref, c_ref, o0_ref, o1_ref, o2_ref, l0_ref, l1_ref, l2_ref,
               mg_ref, wgate_ref, lg_ref, lb_ref, wc_ref, wa_ref, wo_ref, ng_ref, wg_ref, wu_ref, wd_ref,
               fg_ref, out_ref, y_ref, attn_ref, act_ref):
    h_mix = _rmsnorm(x_ref[...], mg_ref[...]).astype(BF16)

    def gate(j):
        return jax.nn.sigmoid(jnp.dot(h_mix, wgate_ref[:, j * D_MODEL:(j + 1) * D_MODEL],
                                      preferred_element_type=F32))

    sgc = gate(0)
    c = c_ref[...]
    mu = jnp.mean(c, axis=-1, keepdims=True)
    d = c - mu
    var = jnp.mean(d * d, axis=-1, keepdims=True)
    y = d * jax.lax.rsqrt(var + EPS) * lg_ref[...] + lb_ref[...]
    y_ref[...] = (y * jax.nn.sigmoid(y)).astype(BF16)
    mixed = sgc * jnp.dot(y_ref[...], wc_ref[...], preferred_element_type=F32)
    sga = gate(1)

    l0, l1, l2 = l0_ref[...], l1_ref[...], l2_ref[...]
    mx = jnp.maximum(jnp.maximum(l0, l1), l2)
    e0, e1, e2 = jnp.exp(l0 - mx), jnp.exp(l1 - mx), jnp.exp(l2 - mx)
    tot = e0 + e1 + e2
    for h in range(HEADS_PER_GROUP):
        pick = slice(h * LSE_LANES, h * LSE_LANES + 1)
        acc = (e0[:, pick] / tot[:, pick]) * o0_ref[h]
        acc = acc + (e1[:, pick] / tot[:, pick]) * o1_ref[h]
        acc = acc + (e2[:, pick] / tot[:, pick]) * o2_ref[h]
        attn_ref[:, h * HEAD_DIM:(h + 1) * HEAD_DIM] = acc.astype(BF16)
    mixed = mixed + sga * jnp.dot(attn_ref[...], wa_ref[...], preferred_element_type=F32)
    x = x_ref[...] + jnp.dot(mixed.astype(BF16), wo_ref[...], preferred_element_type=F32)

    h_in = _rmsnorm(x, ng_ref[...]).astype(BF16)
    for ch in range(D_FF // FF_CHUNK):
        _swiglu_chunk(h_in, wg_ref, wu_ref, act_ref, ch)
    out = x + 0.5 * jnp.dot(act_ref[...], wd_ref[...], preferred_element_type=F32)
    out_ref[...] = _rmsnorm(out, fg_ref[...])


def _tail(x, c, outs, lses, mg, wgate, lg, lb, wc, wa, wo, ng, wg, wu, wd, fg, batch, seq):
    tokens = batch * seq
    tps = seq // TAIL_TILE
    tile = pl.BlockSpec((TAIL_TILE, D_MODEL), lambda i: (i, 0))
    o_tile = pl.BlockSpec((None, HEADS_PER_GROUP, TAIL_TILE, HEAD_DIM), lambda i: (i // tps, 0, i % tps, 0))
    l_tile = pl.BlockSpec((None, TAIL_TILE, LANES), lambda i: (i // tps, i % tps, 0))
    row = _const_spec((1, D_MODEL))
    return pl.pallas_call(
        _tail_body,
        grid=(tokens // TAIL_TILE,),
        in_specs=[tile, tile, o_tile, o_tile, o_tile, l_tile, l_tile, l_tile,
                  row, _const_spec((D_MODEL, 2 * D_MODEL)),
                  row, row, _const_spec((D_MODEL, D_MODEL)), _const_spec((GROUP_WIDTH, D_MODEL)),
                  _const_spec((D_MODEL, D_MODEL)),
                  row, _const_spec((D_MODEL, D_FF)), _const_spec((D_MODEL, D_FF)),
                  _const_spec((D_FF, D_MODEL)), row],
        out_specs=tile,
        out_shape=jax.ShapeDtypeStruct((tokens, D_MODEL), F32),
        scratch_shapes=[pltpu.VMEM((TAIL_TILE, D_MODEL), BF16),
                        pltpu.VMEM((TAIL_TILE, GROUP_WIDTH), BF16),
                        pltpu.VMEM((TAIL_TILE, D_FF), BF16)],
        compiler_params=_params(1),
        name="tail",
    )(x, c, *outs, *lses, mg, wgate, lg, lb, wc, wa, wo, ng, wg, wu, wd, fg)


def kernel(x, ffn1_norm, ffn1_w_gate, ffn1_w_up, ffn1_w_down, mix_norm, w_in, conv_dw_kernel, conv_dw_bias, conv_ln_gain, conv_ln_bias, conv_w_out, attn_w_out, w_o, ffn2_norm, ffn2_w_gate, ffn2_w_up, ffn2_w_down, final_norm):
    batch, seq, _ = x.shape
    depth = ffn1_norm.shape[0]
    assert depth == 1, "the tail kernel applies the final norm, so it closes the single layer"
    assert seq % ATTN_CHUNK == 0 and seq % TOKEN_TILE == 0 and seq % TAIL_TILE == 0
    tokens = batch * seq
    assert tokens % FFN_TILE == 0
    l = 0
    xt = x.reshape(tokens, D_MODEL)
    xt = _ffn(xt, ffn1_norm[l][None], ffn1_w_gate[l].astype(BF16), ffn1_w_up[l].astype(BF16),
              ffn1_w_down[l].astype(BF16))
    w_in_bf16 = w_in[l].astype(BF16)
    gate_base = 2 * D_MODEL + 3 * ATTN_WIDTH
    conv, qkv0, qkv1, qkv2 = _proj(xt, mix_norm[l][None], w_in_bf16[:, :gate_base],
                                   conv_dw_kernel[l], conv_dw_bias[l][None], batch, seq)
    outs, lses = [], []
    for gi, qkv in enumerate((qkv0, qkv1, qkv2)):
        o_g, l_g = _attn_group(qkv, gi, batch, seq)
        outs.append(o_g)
        lses.append(l_g)
    out = _tail(xt, conv.reshape(tokens, D_MODEL), outs, lses, mix_norm[l][None], w_in_bf16[:, gate_base:],
                conv_ln_gain[l][None], conv_ln_bias[l][None],
                conv_w_out[l].astype(BF16), attn_w_out[l].astype(BF16), w_o[l].astype(BF16),
                ffn2_norm[l][None], ffn2_w_gate[l].astype(BF16), ffn2_w_up[l].astype(BF16),
                ffn2_w_down[l].astype(BF16), final_norm[None], batch, seq)
    return out.reshape(batch, seq, D_MODEL)
```
